```python
import math
import jax, jax.numpy as jnp
from jax import lax
import numpy as np

D_MODEL = 1024
BATCH = 4
SEQ = 4096
DEPTH = 2

CHUNK = 64
N_BRANCH = 4
BR_WIDTH = D_MODEL // 4
POOL_WINDOWS = (2, 4, 8, 16)
POOL_GROUPS = len(POOL_WINDOWS)
POOL_GW = BR_WIDTH // POOL_GROUPS
DIFF_HEADS = 4
DIFF_QK_DIM = BR_WIDTH // (2 * DIFF_HEADS)
DIFF_V_DIM = 2 * DIFF_QK_DIM
ROPE_THETA = 10000.0
Q_BLOCK = 128
CONV_WIDTH = 31
SGU_LEN = 128
SGU_GROUPS = 4
SGU_GW = BR_WIDTH // SGU_GROUPS
D_FF = 2816
N_EXPERTS = 8
TOP_K = 2
D_FF_EXPERT = 3584
N_DENSE = (DEPTH + 1) // 2
N_MOE = DEPTH // 2
EPS = 1e-6
NEG_INF = -1e30

IN_A = BR_WIDTH
IN_Q = DIFF_HEADS * 2 * DIFF_QK_DIM
IN_K = DIFF_HEADS * 2 * DIFF_QK_DIM
IN_V = DIFF_HEADS * DIFF_V_DIM
IN_C = 2 * BR_WIDTH
IN_D = 2 * BR_WIDTH
D_IN = IN_A + IN_Q + IN_K + IN_V + IN_C + IN_D

kernel_name = "hybrid_gated_pool_diffattn_conv_sgu_moe"


def rmsnorm(x, g):
    xf = x.astype(jnp.float32)
    y = xf * lax.rsqrt(jnp.mean(xf * xf, axis=-1, keepdims=True) + EPS)
    return (y * g.astype(jnp.float32)).astype(x.dtype)


def layernorm(x, g, b):
    xf = x.astype(jnp.float32)
    mu = jnp.mean(xf, axis=-1, keepdims=True)
    xc = xf - mu
    y = xc * lax.rsqrt(jnp.mean(xc * xc, axis=-1, keepdims=True) + EPS)
    return (y * g.astype(jnp.float32) + b.astype(jnp.float32)).astype(x.dtype)


def rope_tables(seq, dim):
    pos = jnp.arange(seq, dtype=jnp.float32)
    inv = 1.0 / (ROPE_THETA ** (jnp.arange(0, dim, 2, dtype=jnp.float32) / dim))
    ang = pos[:, None] * inv[None, :]
    return jnp.cos(ang), jnp.sin(ang)


def apply_rope(x, cos, sin):
    c = cos[None, :, None, None, :].astype(x.dtype)
    s = sin[None, :, None, None, :].astype(x.dtype)
    x1, x2 = jnp.split(x, 2, axis=-1)
    return jnp.concatenate([x1 * c - x2 * s, x2 * c + x1 * s], axis=-1)


def pool_mixer(a, pool_w, pool_scale):
    B, S, C = a.shape
    af = a.astype(jnp.float32)
    cs = jnp.concatenate([jnp.zeros((B, 1, C), jnp.float32), jnp.cumsum(af, axis=1)], axis=1)
    t = jnp.arange(S)
    outs = []
    for g, w in enumerate(POOL_WINDOWS):
        lo = jnp.maximum(t + 1 - w, 0)
        csg = cs[:, :, g * POOL_GW:(g + 1) * POOL_GW]
        win_sum = csg[:, 1:] - csg[:, lo]
        cnt = (t + 1 - lo).astype(jnp.float32)[None, :, None]
        outs.append(win_sum / cnt - af[..., g * POOL_GW:(g + 1) * POOL_GW])
    p = jnp.stack(outs, axis=2).astype(a.dtype)
    y = jnp.einsum('bsgc,gcd->bsgd', p, pool_w)
    return y.reshape(B, S, C) * pool_scale


def diff_attention(q, k, v, qn_g, kn_g, lq1, lk1, lq2, lk2, subln_g, layer_idx, cos, sin):
    B, S, _ = q.shape
    q = q.reshape(B, S, DIFF_HEADS, 2, DIFF_QK_DIM)
    k = k.reshape(B, S, DIFF_HEADS, 2, DIFF_QK_DIM)
    v = v.reshape(B, S, DIFF_HEADS, DIFF_V_DIM).transpose(0, 2, 1, 3)
    q = apply_rope(rmsnorm(q, qn_g), cos, sin).transpose(0, 3, 2, 1, 4)
    k = apply_rope(rmsnorm(k, kn_g), cos, sin).transpose(0, 3, 2, 1, 4)
    lam_init = 0.8 - 0.6 * math.exp(-0.3 * layer_idx)
    lam = (jnp.exp(jnp.sum(lq1.astype(jnp.float32) * lk1.astype(jnp.float32)))
           - jnp.exp(jnp.sum(lq2.astype(jnp.float32) * lk2.astype(jnp.float32))) + lam_init)
    scale = DIFF_QK_DIM ** -0.5
    chunk_id = jnp.arange(S) // CHUNK
    outs = []
    for qb in range(S // Q_BLOCK):
        q0 = qb * Q_BLOCK
        kend = q0 + Q_BLOCK
        s = jnp.einsum('bchqd,bchkd->bchqk', q[:, :, :, q0:kend], k[:, :, :, :kend]).astype(jnp.float32) * scale
        mask = chunk_id[q0:kend, None] >= chunk_id[None, :kend]
        p = jax.nn.softmax(jnp.where(mask, s, NEG_INF), axis=-1)
        a = p[:, 0] - lam * p[:, 1]
        outs.append(jnp.einsum('bhqk,bhkv->bhqv', a.astype(v.dtype), v[:, :, :kend]))
    o = jnp.concatenate(outs, axis=2)
    o = rmsnorm(o, subln_g) * (1.0 - lam_init)
    return o.transpose(0, 2, 1, 3).reshape(B, S, DIFF_HEADS * DIFF_V_DIM)


def conv_module(c_in, conv_w, conv_b, ln_g, ln_b):
    a, gt = jnp.split(c_in, 2, axis=-1)
    u = a * jax.nn.sigmoid(gt)
    y = lax.conv_general_dilated(
        u, conv_w[:, None, :], window_strides=(1,), padding=[(CONV_WIDTH - 1, 0)],
        dimension_numbers=('NWC', 'WIO', 'NWC'), feature_group_count=BR_WIDTH) + conv_b
    return jax.nn.silu(layernorm(y, ln_g, ln_b))


def spatial_gating(d_in, ln_g, ln_b, w_s, b_s):
    z = jax.nn.gelu(d_in, approximate=False)
    u, v = jnp.split(z, 2, axis=-1)
    v = layernorm(v, ln_g, ln_b)
    B, S, _ = v.shape
    v = v.reshape(B, S // SGU_LEN, SGU_LEN, SGU_GROUPS, SGU_GW)
    tri = jnp.tril(jnp.ones((SGU_LEN, SGU_LEN), dtype=bool))
    w = jnp.where(tri[None], w_s, jnp.zeros_like(w_s))
    sv = jnp.einsum('gts,bnsgc->bntgc', w, v) + b_s.T[None, None, :, :, None]
    return u * sv.reshape(B, S, BR_WIDTH)


def swiglu(h, w1, w3, w2):
    return (jax.nn.silu(h @ w1) * (h @ w3)) @ w2


def moe_ffn(h, router, w1, w3, w2):
    B, S, D = h.shape
    t = h.reshape(B * S, D)
    logits = (t @ router).astype(jnp.float32)
    top_v, top_i = lax.top_k(logits, TOP_K)
    top_w = jax.nn.softmax(top_v, axis=-1)
    gates = jnp.sum(jax.nn.one_hot(top_i, N_EXPERTS, dtype=jnp.float32) * top_w[..., None], axis=1)
    gates = gates.astype(t.dtype)
    out = jnp.zeros_like(t)
    for e in range(N_EXPERTS):
        out = out + gates[:, e:e + 1] * swiglu(t, w1[e], w3[e], w2[e])
    return out.reshape(B, S, D)


def setup_inputs(seed: int = 0) -> dict:
    key = jax.random.key(seed)
    ks = iter(jax.random.split(key, 40))

    def nrm(shape, scale):
        return jax.random.normal(next(ks), shape, jnp.float32) * scale

    def gain(shape):
        return 1.0 + nrm(shape, 0.02)

    L = DEPTH
    return {
        "x": nrm((BATCH, SEQ, D_MODEL), 1.0),
        "norm1_g": gain((L, D_MODEL)),
        "w_in": nrm((L, D_MODEL, D_IN), D_MODEL ** -0.5),
        "w_gate": nrm((L, D_MODEL, N_BRANCH * D_MODEL), D_MODEL ** -0.5),
        "b_gate": nrm((L, N_BRANCH * D_MODEL), 0.02),
        "w_branch": nrm((L, N_BRANCH, BR_WIDTH, D_MODEL), BR_WIDTH ** -0.5),
        "w_out": nrm((L, D_MODEL, D_MODEL), D_MODEL ** -0.5),
        "pool_w": nrm((L, POOL_GROUPS, POOL_GW, POOL_GW), POOL_GW ** -0.5),
        "pool_scale": gain((L, BR_WIDTH)),
        "attn_qn_g": gain((L, DIFF_QK_DIM)),
        "attn_kn_g": gain((L, DIFF_QK_DIM)),
        "lam_q1": nrm((L, DIFF_QK_DIM), 0.1),
        "lam_k1": nrm((L, DIFF_QK_DIM), 0.1),
        "lam_q2": nrm((L, DIFF_QK_DIM), 0.1),
        "lam_k2": nrm((L, DIFF_QK_DIM), 0.1),
        "attn_subln_g": gain((L, DIFF_V_DIM)),
        "conv_w": nrm((L, CONV_WIDTH, BR_WIDTH), CONV_WIDTH ** -0.5),
        "conv_b": nrm((L, BR_WIDTH), 0.02),
        "conv_ln_g": gain((L, BR_WIDTH)),
        "conv_ln_b": nrm((L, BR_WIDTH), 0.02),
        "sgu_ln_g": gain((L, BR_WIDTH)),
        "sgu_ln_b": nrm((L, BR_WIDTH), 0.02),
        "sgu_w": nrm((L, SGU_GROUPS, SGU_LEN, SGU_LEN), SGU_LEN ** -0.5),
        "sgu_b": gain((L, SGU_GROUPS, SGU_LEN)),
        "norm2_g": gain((L, D_MODEL)),
        "ffn_w1": nrm((N_DENSE, D_MODEL, D_FF), D_MODEL ** -0.5),
        "ffn_w3": nrm((N_DENSE, D_MODEL, D_FF), D_MODEL ** -0.5),
        "ffn_w2": nrm((N_DENSE, D_FF, D_MODEL), D_FF ** -0.5),
        "moe_router": nrm((N_MOE, D_MODEL, N_EXPERTS), D_MODEL ** -0.5),
        "moe_w1": nrm((N_MOE, N_EXPERTS, D_MODEL, D_FF_EXPERT), D_MODEL ** -0.5),
        "moe_w3": nrm((N_MOE, N_EXPERTS, D_MODEL, D_FF_EXPERT), D_MODEL ** -0.5),
        "moe_w2": nrm((N_MOE, N_EXPERTS, D_FF_EXPERT, D_MODEL), D_FF_EXPERT ** -0.5),
    }


def reference(x, norm1_g, w_in, w_gate, b_gate, w_branch, w_out, pool_w, pool_scale,
              attn_qn_g, attn_kn_g, lam_q1, lam_k1, lam_q2, lam_k2, attn_subln_g,
              conv_w, conv_b, conv_ln_g, conv_ln_b, sgu_ln_g, sgu_ln_b, sgu_w, sgu_b,
              norm2_g, ffn_w1, ffn_w3, ffn_w2, moe_router, moe_w1, moe_w3, moe_w2):
    B, S, D = x.shape
    cos, sin = rope_tables(S, DIFF_QK_DIM)
    o_q = IN_A
    o_k = o_q + IN_Q
    o_v = o_k + IN_K
    o_c = o_v + IN_V
    o_d = o_c + IN_C
    for l in range(DEPTH):
        h = rmsnorm(x, norm1_g[l])
        proj = h @ w_in[l]
        ya = pool_mixer(proj[..., :o_q], pool_w[l], pool_scale[l])
        yb = diff_attention(proj[..., o_q:o_k], proj[..., o_k:o_v], proj[..., o_v:o_c],
                            attn_qn_g[l], attn_kn_g[l], lam_q1[l], lam_k1[l], lam_q2[l], lam_k2[l],
                            attn_subln_g[l], l, cos, sin)
        yc = conv_module(proj[..., o_c:o_d], conv_w[l], conv_b[l], conv_ln_g[l], conv_ln_b[l])
        yd = spatial_gating(proj[..., o_d:], sgu_ln_g[l], sgu_ln_b[l], sgu_w[l], sgu_b[l])
        br = jnp.stack([ya, yb, yc, yd], axis=2)
        br = jnp.einsum('bsic,icd->bsid', br, w_branch[l])
        gates = jax.nn.sigmoid(h @ w_gate[l] + b_gate[l]).reshape(B, S, N_BRANCH, D)
        merged = jnp.sum(gates * br, axis=2)
        x = x + merged @ w_out[l]
        h2 = rmsnorm(x, norm2_g[l])
        if l % 2 == 0:
            i = l // 2
            x = x + swiglu(h2, ffn_w1[i], ffn_w3[i], ffn_w2[i])
        else:
            i = l // 2
            x = x + moe_ffn(h2, moe_router[i], moe_w1[i], moe_w3[i], moe_w2[i])
    return x
```

```python
import functools
import math

import jax
import jax.numpy as jnp
from jax import lax
from jax.experimental import pallas as pl
from jax.experimental.pallas import tpu as pltpu

F32 = jnp.float32
BF16 = jnp.bfloat16

N_BRANCH = 4
BR = 256
POOL_WINDOWS = (2, 4, 8, 16)
POOL_GW = 64
HEADS = 4
QK_DIM = 32
V_DIM = 64
ROPE_THETA = 10000.0
CHUNK = 64
CONV_W = 31
SGU_LEN = 128
SGU_GROUPS = 4
N_EXPERTS = 8
EPS = 1e-6
NEG_INF = -1e30

HALO = 32
V7X_VMEM_LIMIT = 56 * 1024 * 1024


def _rms(x, g):
    return x * lax.rsqrt(jnp.mean(x * x, axis=-1, keepdims=True) + EPS) * g


def _ln(x, g, b):
    mu = jnp.mean(x, axis=-1, keepdims=True)
    xc = x - mu
    return xc * lax.rsqrt(jnp.mean(xc * xc, axis=-1, keepdims=True) + EPS) * g + b


def _sigmoid(x):
    return 1.0 / (1.0 + jnp.exp(-x))


def _dot(a, b):
    return jnp.dot(a, b, preferred_element_type=F32)


def _mixer_in_kernel(x_ref, xh_ref, g1_ref, win_ref, cos_ref, sin_ref, qg_ref, kg_ref, bd32_ref,
                     poolw_ref, pools_ref, convw_ref, convb_ref, clng_ref, clnb_ref,
                     slng_ref, slnb_ref, sguw_ref, sgub_ref,
                     q_ref, kt_ref, v_ref, ya_ref, yc_ref, yd_ref,
                     a_scr, s2_scr, s4_scr, s8_scr, u_scr):
    i = pl.program_id(1)
    tm = x_ref.shape[1]
    g1 = g1_ref[...]
    h = _rms(x_ref[0], g1).astype(BF16)
    proj = _dot(h, win_ref[...])

    hh = _rms(xh_ref[0], g1).astype(BF16)
    halo_ok = (i > 0).astype(F32)
    pa_h = _dot(hh, win_ref[:, 0:BR]) * halo_ok
    pc_h = _dot(hh, win_ref[:, 4 * BR:6 * BR]) * halo_ok

    a = proj[:, 0:BR]
    a_scr[0:HALO, :] = pa_h
    a_scr[HALO:HALO + tm, :] = a
    n_ext = tm + HALO
    s2_scr[8:n_ext, :] = a_scr[8:n_ext, :] + a_scr[7:n_ext - 1, :]
    s4_scr[16:n_ext, :] = s2_scr[16:n_ext, :] + s2_scr[14:n_ext - 2, :]
    s8_scr[24:n_ext, :] = s4_scr[24:n_ext, :] + s4_scr[20:n_ext - 4, :]
    s16 = s8_scr[HALO:n_ext, :] + s8_scr[HALO - 8:n_ext - 8, :]
    lane = lax.broadcasted_iota(jnp.int32, (tm, BR), 1)
    row = lax.broadcasted_iota(jnp.int32, (tm, BR), 0)
    win_sum = jnp.where(lane < POOL_GW, s2_scr[HALO:n_ext, :],
                        jnp.where(lane < 2 * POOL_GW, s4_scr[HALO:n_ext, :],
                                  jnp.where(lane < 3 * POOL_GW, s8_scr[HALO:n_ext, :], s16)))
    width = jnp.where(lane < POOL_GW, POOL_WINDOWS[0],
                      jnp.where(lane < 2 * POOL_GW, POOL_WINDOWS[1],
                                jnp.where(lane < 3 * POOL_GW, POOL_WINDOWS[2], POOL_WINDOWS[3])))
    cnt = jnp.minimum(width, i * tm + row + 1).astype(F32)
    pooled = win_sum / cnt - a
    ya_ref[0] = (_dot(pooled.astype(BF16), poolw_ref[...]) * pools_ref[...]).astype(BF16)

    lo_half = (lane % QK_DIM) < (QK_DIM // 2)

    def qk_norm_rope(t, g_ref, scale):
        ss = _dot((t * t).astype(BF16), bd32_ref[...])
        tn = t * lax.rsqrt(ss * (1.0 / QK_DIM) + EPS) * g_ref[...]
        rot = jnp.where(lo_half, pltpu.roll(tn, BR - QK_DIM // 2, 1), pltpu.roll(tn, QK_DIM // 2, 1))
        return (tn * cos_ref[...] + rot * sin_ref[...]) * scale

    q_ref[0] = qk_norm_rope(proj[:, BR:2 * BR], qg_ref, QK_DIM ** -0.5).astype(BF16)
    kt_ref[0] = qk_norm_rope(proj[:, 2 * BR:3 * BR], kg_ref, 1.0).T.astype(BF16)
    v = proj[:, 3 * BR:4 * BR].astype(BF16)
    for hd in range(HEADS):
        v_ref[0, hd] = v[:, hd * V_DIM:(hd + 1) * V_DIM]

    u_scr[0:HALO, :] = pc_h[:, 0:BR] * _sigmoid(pc_h[:, BR:2 * BR])
    u_scr[HALO:HALO + tm, :] = proj[:, 4 * BR:5 * BR] * _sigmoid(proj[:, 5 * BR:6 * BR])
    taps = [convw_ref[j:j + 1, :] for j in range(CONV_W)]
    convb = convb_ref[...]
    clng = clng_ref[...]
    clnb = clnb_ref[...]
    rc = 64
    base = HALO - (CONV_W - 1)
    for c in range(tm // rc):
        acc = convb + taps[0] * u_scr[c * rc + base:c * rc + base + rc, :]
        for j in range(1, CONV_W):
            acc = acc + taps[j] * u_scr[c * rc + base + j:c * rc + base + j + rc, :]
        y = _ln(acc, clng, clnb)
        yc_ref[0, c * rc:(c + 1) * rc, :] = (y * _sigmoid(y)).astype(BF16)

    d_in = proj[:, 6 * BR:8 * BR]
    z = 0.5 * d_in * (1.0 + lax.erf(d_in * (2.0 ** -0.5)))
    vv = _ln(z[:, BR:2 * BR], slng_ref[...], slnb_ref[...]).astype(BF16)
    tr = lax.broadcasted_iota(jnp.int32, (SGU_LEN, SGU_LEN), 0)
    tc = lax.broadcasted_iota(jnp.int32, (SGU_LEN, SGU_LEN), 1)
    wtri = [jnp.where(tr >= tc, sguw_ref[g], 0.0).astype(BF16) for g in range(SGU_GROUPS)]
    lane_g = lax.broadcasted_iota(jnp.int32, (SGU_LEN, BR), 1) // (BR // SGU_GROUPS)
    sgub = sgub_ref[...]
    for n in range(tm // SGU_LEN):
        vn = vv[n * SGU_LEN:(n + 1) * SGU_LEN, :]
        sv = _dot(wtri[0], vn)
        for g in range(1, SGU_GROUPS):
            sv = jnp.where(lane_g == g, _dot(wtri[g], vn), sv)
        yd_ref[0, n * SGU_LEN:(n + 1) * SGU_LEN, :] = (
            z[n * SGU_LEN:(n + 1) * SGU_LEN, 0:BR] * (sv + sgub)).astype(BF16)


def _mixer_in(x3, g1, win, cos_t, sin_t, qg, kg, bd32, poolw, pools, convw, convb, clng, clnb,
              slng, slnb, sguw, sgub, tm):
    B, S, D = x3.shape
    nt = S // tm
    hb = tm // HALO
    full = lambda *shape: pl.BlockSpec(shape, lambda b, i: (0,) * len(shape))
    in_specs = [
        pl.BlockSpec((1, tm, D), lambda b, i: (b, i, 0)),
        pl.BlockSpec((1, HALO, D), lambda b, i: (b, jnp.maximum(i * hb - 1, 0), 0)),
        full(1, D), full(D, 8 * BR),
        pl.BlockSpec((tm, BR), lambda b, i: (i, 0)),
        pl.BlockSpec((tm, BR), lambda b, i: (i, 0)),
        full(1, BR), full(1, BR), full(BR, BR),
        full(BR, BR), full(1, BR), full(CONV_W, BR), full(1, BR), full(1, BR), full(1, BR),
        full(1, BR), full(1, BR), full(SGU_GROUPS, SGU_LEN, SGU_LEN), full(SGU_LEN, BR),
    ]
    tok = lambda: pl.BlockSpec((1, tm, BR), lambda b, i: (b, i, 0))
    out_specs = [
        tok(),
        pl.BlockSpec((1, BR, tm), lambda b, i: (b, 0, i)),
        pl.BlockSpec((1, HEADS, tm, V_DIM), lambda b, i: (b, 0, i, 0)),
        tok(), tok(), tok(),
    ]
    out_shape = [
        jax.ShapeDtypeStruct((B, S, BR), BF16),
        jax.ShapeDtypeStruct((B, BR, S), BF16),
        jax.ShapeDtypeStruct((B, HEADS, S, V_DIM), BF16),
        jax.ShapeDtypeStruct((B, S, BR), BF16),
        jax.ShapeDtypeStruct((B, S, BR), BF16),
        jax.ShapeDtypeStruct((B, S, BR), BF16),
    ]
    ext = pltpu.VMEM((tm + HALO, BR), F32)
    return pl.pallas_call(
        _mixer_in_kernel,
        grid=(B, nt),
        in_specs=in_specs,
        out_specs=out_specs,
        out_shape=out_shape,
        scratch_shapes=[ext, ext, ext, ext, ext],
        compiler_params=pltpu.CompilerParams(
            dimension_semantics=("parallel", "parallel"), vmem_limit_bytes=V7X_VMEM_LIMIT),
        name="mixer_in",
    )(x3, x3, g1, win, cos_t, sin_t, qg, kg, bd32, poolw, pools, convw, convb, clng, clnb,
      slng, slnb, sguw, sgub)


def _attn_kernel(lam_ref, q_ref, kt_ref, v_ref, sg_ref, o_ref, *, lam_init, tq):
    i = pl.program_id(1)
    lamv = lam_ref[...]
    lam = (jnp.exp(jnp.sum(lamv[0:1] * lamv[1:2], axis=-1, keepdims=True))
           - jnp.exp(jnp.sum(lamv[2:3] * lamv[3:4], axis=-1, keepdims=True)) + lam_init)
    q = q_ref[0]
    r_chunk = lax.broadcasted_iota(jnp.int32, (tq, tq), 0) // CHUNK
    c_chunk = lax.broadcasted_iota(jnp.int32, (tq, tq), 1) // CHUNK
    diag_mask = r_chunk >= c_chunk
    sg = sg_ref[...]
    heads = []
    for hd in range(HEADS):
        comps = []
        for c in range(2):
            g = 2 * hd + c
            qg = q[:, g * QK_DIM:(g + 1) * QK_DIM]

            def step(j, carry, masked):
                m, l, acc = carry
                k0 = pl.multiple_of(j * tq, tq)
                s = _dot(qg, kt_ref[0, g * QK_DIM:(g + 1) * QK_DIM, pl.ds(k0, tq)])
                if masked:
                    s = jnp.where(diag_mask, s, NEG_INF)
                m_new = jnp.maximum(m, jnp.max(s, axis=-1, keepdims=True))
                p = jnp.exp(s - m_new)
                alpha = jnp.exp(m - m_new)
                l = alpha * l + jnp.sum(p, axis=-1, keepdims=True)
                acc = alpha * acc + _dot(p.astype(BF16), v_ref[0, hd, pl.ds(k0, tq), :])
                return m_new, l, acc

            init = (jnp.full((tq, 1), NEG_INF, F32), jnp.zeros((tq, 1), F32), jnp.zeros((tq, V_DIM), F32))
            carry = lax.fori_loop(0, i, functools.partial(step, masked=False), init)
            _, l, acc = step(i, carry, True)
            comps.append(acc / l)
        o = comps[0] - lam * comps[1]
        o = o * lax.rsqrt(jnp.mean(o * o, axis=-1, keepdims=True) + EPS) * sg * (1.0 - lam_init)
        heads.append(o)
    o_ref[0] = jnp.concatenate(heads, axis=-1).astype(BF16)


def _attention(lamv, q, kt, v, sg, lam_init, tq):
    B, S, _ = q.shape
    return pl.pallas_call(
        functools.partial(_attn_kernel, lam_init=lam_init, tq=tq),
        grid=(B, S // tq),
        in_specs=[
            pl.BlockSpec((4, QK_DIM), lambda b, i: (0, 0)),
            pl.BlockSpec((1, tq, BR), lambda b, i: (b, i, 0)),
            pl.BlockSpec((1, BR, S), lambda b, i: (b, 0, 0)),
            pl.BlockSpec((1, HEADS, S, V_DIM), lambda b, i: (b, 0, 0, 0)),
            pl.BlockSpec((1, V_DIM), lambda b, i: (0, 0)),
        ],
        out_specs=pl.BlockSpec((1, tq, BR), lambda b, i: (b, i, 0)),
        out_shape=jax.ShapeDtypeStruct((B, S, BR), BF16),
        compiler_params=pltpu.CompilerParams(
            dimension_semantics=("parallel", "parallel"), vmem_limit_bytes=V7X_VMEM_LIMIT),
        name="diff_attn",
    )(lamv, q, kt, v, sg)


def _merge_kernel(*refs, with_router):
    if with_router:
        (x_ref, ya_ref, yb_ref, yc_ref, yd_ref, g1_ref, wg_ref, bg_ref, wb_ref, wo_ref, g2_ref, rt_ref,
         x1_ref, h2_ref, gates_ref) = refs
    else:
        (x_ref, ya_ref, yb_ref, yc_ref, yd_ref, g1_ref, wg_ref, bg_ref, wb_ref, wo_ref, g2_ref,
         x1_ref, h2_ref) = refs
    x = x_ref[...]
    d = x.shape[1]
    h = _rms(x, g1_ref[...]).astype(BF16)
    merged = None
    for b, y_ref in enumerate((ya_ref, yb_ref, yc_ref, yd_ref)):
        gate = _sigmoid(_dot(h, wg_ref[:, b * d:(b + 1) * d]) + bg_ref[:, b * d:(b + 1) * d])
        term = gate * _dot(y_ref[...], wb_ref[b])
        merged = term if merged is None else merged + term
    x1 = x + _dot(merged.astype(BF16), wo_ref[...])
    x1_ref[...] = x1
    h2 = _rms(x1, g2_ref[...])
    h2_ref[...] = h2.astype(BF16)
    if with_router:
        logits = lax.dot_general(rt_ref[...], h2, (((1,), (1,)), ((), ())),
                                 preferred_element_type=F32, precision=lax.Precision.HIGHEST)
        e_idx = lax.broadcasted_iota(jnp.int32, logits.shape, 0)
        v1 = jnp.max(logits, axis=0, keepdims=True)
        i1 = jnp.min(jnp.where(logits == v1, e_idx, N_EXPERTS), axis=0, keepdims=True)
        rest = jnp.where(e_idx == i1, -jnp.inf, logits)
        v2 = jnp.max(rest, axis=0, keepdims=True)
        i2 = jnp.min(jnp.where(rest == v2, e_idx, N_EXPERTS), axis=0, keepdims=True)
        w2 = 1.0 / (1.0 + jnp.exp(v1 - v2))
        gates_ref[...] = jnp.where(e_idx == i1, 1.0 - w2, jnp.where(e_idx == i2, w2, 0.0))


def _merge(x2, ya, yb, yc, yd, g1, wg, bg, wb, wo, g2, router_t, tm):
    T, D = x2.shape
    with_router = router_t is not None
    full = lambda *shape: pl.BlockSpec(shape, lambda i: (0,) * len(shape))
    tokb = lambda w: pl.BlockSpec((tm, w), lambda i: (i, 0))
    in_specs = [tokb(D), tokb(BR), tokb(BR), tokb(BR), tokb(BR),
                full(1, D), full(D, N_BRANCH * D), full(1, N_BRANCH * D), full(N_BRANCH, BR, D),
                full(D, D), full(1, D)]
    args = [x2, ya, yb, yc, yd, g1, wg, bg, wb, wo, g2]
    out_specs = [tokb(D), tokb(D)]
    out_shape = [jax.ShapeDtypeStruct((T, D), F32), jax.ShapeDtypeStruct((T, D), BF16)]
    if with_router:
        in_specs.append(full(N_EXPERTS, D))
        args.append(router_t)
        out_specs.append(pl.BlockSpec((N_EXPERTS, tm), lambda i: (0, i)))
        out_shape.append(jax.ShapeDtypeStruct((N_EXPERTS, T), F32))
    return pl.pallas_call(
        functools.partial(_merge_kernel, with_router=with_router),
        grid=(T // tm,),
        in_specs=in_specs,
        out_specs=out_specs,
        out_shape=out_shape,
        compiler_params=pltpu.CompilerParams(
            dimension_semantics=("parallel",), vmem_limit_bytes=V7X_VMEM_LIMIT),
        name="merge_router" if with_router else "merge",
    )(*args)


def _ffn_kernel(*refs, gated):
    if gated:
        x1_ref, h2_ref, gate_ref, w1_ref, w3_ref, w2_ref, o_ref = refs
    else:
        x1_ref, h2_ref, w1_ref, w3_ref, w2_ref, o_ref = refs

    @pl.when((pl.program_id(1) == 0) & (pl.program_id(2) == 0))
    def _():
        o_ref[...] = x1_ref[...]

    h2 = h2_ref[...]
    a = _dot(h2, w1_ref[0])
    mid = a * _sigmoid(a) * _dot(h2, w3_ref[0])
    if gated:
        mid = mid * gate_ref[0]
    o_ref[...] += _dot(mid.astype(BF16), w2_ref[0])


def _ffn(x1, h2, gate_cols, w1, w3, w2, tm, tf):
    T, D = x1.shape
    E, _, F = w1.shape
    gated = gate_cols is not None
    in_specs = [pl.BlockSpec((tm, D), lambda i, e, f: (i, 0)),
                pl.BlockSpec((tm, D), lambda i, e, f: (i, 0))]
    args = [x1, h2]
    if gated:
        in_specs.append(pl.BlockSpec((1, tm, 1), lambda i, e, f: (e, i, 0)))
        args.append(gate_cols)
    in_specs += [pl.BlockSpec((1, D, tf), lambda i, e, f: (e, 0, f)),
                 pl.BlockSpec((1, D, tf), lambda i, e, f: (e, 0, f)),
                 pl.BlockSpec((1, tf, D), lambda i, e, f: (e, f, 0))]
    args += [w1, w3, w2]
    return pl.pallas_call(
        functools.partial(_ffn_kernel, gated=gated),
        grid=(T // tm, E, F // tf),
        in_specs=in_specs,
        out_specs=pl.BlockSpec((tm, D), lambda i, e, f: (i, 0)),
        out_shape=jax.ShapeDtypeStruct((T, D), F32),
        compiler_params=pltpu.CompilerParams(
            dimension_semantics=("parallel", "arbitrary", "arbitrary"), vmem_limit_bytes=V7X_VMEM_LIMIT),
        name="moe_ffn" if gated else "dense_ffn",
    )(*args)


def _rope_tables(seq):
    pos = jnp.arange(seq, dtype=F32)
    inv = 1.0 / (ROPE_THETA ** (jnp.arange(0, QK_DIM, 2, dtype=F32) / QK_DIM))
    ang = pos[:, None] * inv[None, :]
    cos, sin = jnp.cos(ang), jnp.sin(ang)
    groups = BR // QK_DIM
    cos_t = jnp.tile(jnp.concatenate([cos, cos], axis=-1), (1, groups))
    sin_t = jnp.tile(jnp.concatenate([-sin, sin], axis=-1), (1, groups))
    return cos_t, sin_t


def _block_diag(blocks):
    g, n, m = blocks.shape
    eye = jnp.eye(g, dtype=blocks.dtype)
    return (eye[:, None, :, None] * blocks[:, :, None, :]).reshape(g * n, g * m)


def kernel(x, norm1_g, w_in, w_gate, b_gate, w_branch, w_out, pool_w, pool_scale, attn_qn_g, attn_kn_g,
           lam_q1, lam_k1, lam_q2, lam_k2, attn_subln_g, conv_w, conv_b, conv_ln_g, conv_ln_b,
           sgu_ln_g, sgu_ln_b, sgu_w, sgu_b, norm2_g, ffn_w1, ffn_w3, ffn_w2, moe_router,
           moe_w1, moe_w3, moe_w2):
    B, S, D = x.shape
    depth = w_in.shape[0]
    T = B * S
    tm_mix = min(512, S)
    tq = min(256, S)
    tm_merge = min(512, T)
    tm_ffn = min(1024, T)

    cos_t, sin_t = _rope_tables(S)
    bd32 = _block_diag(jnp.ones((BR // QK_DIM, QK_DIM, QK_DIM), BF16))
    row = lambda v: v.reshape(1, -1).astype(F32)

    for l in range(depth):
        x3 = x.reshape(B, S, D)
        q, kt, v, ya, yc, yd = _mixer_in(
            x3, row(norm1_g[l]), w_in[l].astype(BF16), cos_t, sin_t,
            row(jnp.tile(attn_qn_g[l], BR // QK_DIM)), row(jnp.tile(attn_kn_g[l], BR // QK_DIM)), bd32,
            _block_diag(pool_w[l]).astype(BF16), row(pool_scale[l]),
            conv_w[l], row(conv_b[l]), row(conv_ln_g[l]), row(conv_ln_b[l]),
            row(sgu_ln_g[l]), row(sgu_ln_b[l]), sgu_w[l],
            jnp.repeat(sgu_b[l].T, BR // SGU_GROUPS, axis=1), tm_mix)
        lamv = jnp.stack([lam_q1[l], lam_k1[l], lam_q2[l], lam_k2[l]]).astype(F32)
        lam_init = 0.8 - 0.6 * math.exp(-0.3 * l)
        yb = _attention(lamv, q, kt, v, row(attn_subln_g[l]), lam_init, tq)

        is_moe = l % 2 == 1
        router_t = moe_router[l // 2].T.astype(F32) if is_moe else None
        flat = lambda y: y.reshape(T, BR)
        outs = _merge(x.reshape(T, D), flat(ya), flat(yb), flat(yc), flat(yd), row(norm1_g[l]),
                      w_gate[l].astype(BF16), row(b_gate[l]), w_branch[l].astype(BF16),
                      w_out[l].astype(BF16), row(norm2_g[l]), router_t, tm_merge)
        if is_moe:
            x1, h2, gates_t = outs
            i = l // 2
            x = _ffn(x1, h2, gates_t.reshape(N_EXPERTS, T, 1), moe_w1[i].astype(BF16),
                     moe_w3[i].astype(BF16), moe_w2[i].astype(BF16), tm_ffn, 512)
        else:
            x1, h2 = outs
            i = l // 2
            x = _ffn(x1, h2, None, ffn_w1[i][None].astype(BF16), ffn_w3[i][None].astype(BF16),
                     ffn_w2[i][None].astype(BF16), tm_ffn, ffn_w1.shape[2] // 2)
    return x.reshape(B, S, D)
```

```python
import functools
import math

import jax
import jax.numpy as jnp
from jax import lax
from jax.experimental import pallas as pl
from jax.experimental.pallas import tpu as pltpu

F32 = jnp.float32
BF16 = jnp.bfloat16

N_BRANCH = 4
BR = 256
POOL_WINDOWS = (2, 4, 8, 16)
POOL_GW = 64
HEADS = 4
QK_DIM = 32
V_DIM = 64
ROPE_THETA = 10000.0
CHUNK = 64
CONV_W = 31
SGU_LEN = 128
SGU_GROUPS = 4
N_EXPERTS = 8
EPS = 1e-6
QK_GROUPS = 2 * HEADS
LOG2E = math.log2(math.e)
NEG_INF = -1e30

HALO = 32
V7X_VMEM_LIMIT = 56 * 1024 * 1024


def _rms(x, g):
    return x * lax.rsqrt(jnp.mean(x * x, axis=-1, keepdims=True) + EPS) * g


def _ln(x, g, b):
    mu = jnp.mean(x, axis=-1, keepdims=True)
    xc = x - mu
    return xc * lax.rsqrt(jnp.mean(xc * xc, axis=-1, keepdims=True) + EPS) * g + b


def _sigmoid(x):
    return 1.0 / (1.0 + jnp.exp(-x))


def _dot(a, b):
    return jnp.dot(a, b, preferred_element_type=F32)


def _mixer_in_kernel(x_ref, xh_ref, g1_ref, win_ref, cos_ref, sin_ref, qg_ref, kg_ref, bd32_ref,
                     poolw_ref, pools_ref, convw_ref, convb_ref, clng_ref, clnb_ref,
                     slng_ref, slnb_ref, sguw_ref, sgub_ref,
                     q_ref, k_ref, vt_ref, ya_ref, yc_ref, yd_ref,
                     a_scr, s2_scr, s4_scr, s8_scr, u_scr):
    i = pl.program_id(1)
    tm = x_ref.shape[1]
    g1 = g1_ref[...]
    h = _rms(x_ref[0], g1).astype(BF16)
    proj = _dot(h, win_ref[...])

    hh = _rms(xh_ref[0], g1).astype(BF16)
    halo_ok = (i > 0).astype(F32)
    pa_h = _dot(hh, win_ref[:, 0:BR]) * halo_ok
    pc_h = _dot(hh, win_ref[:, 4 * BR:6 * BR]) * halo_ok

    a = proj[:, 0:BR]
    a_scr[0:HALO, :] = pa_h
    a_scr[HALO:HALO + tm, :] = a
    n_ext = tm + HALO
    s2_scr[8:n_ext, :] = a_scr[8:n_ext, :] + a_scr[7:n_ext - 1, :]
    s4_scr[16:n_ext, :] = s2_scr[16:n_ext, :] + s2_scr[14:n_ext - 2, :]
    s8_scr[24:n_ext, :] = s4_scr[24:n_ext, :] + s4_scr[20:n_ext - 4, :]
    s16 = s8_scr[HALO:n_ext, :] + s8_scr[HALO - 8:n_ext - 8, :]
    lane = lax.broadcasted_iota(jnp.int32, (tm, BR), 1)
    row = lax.broadcasted_iota(jnp.int32, (tm, BR), 0)
    win_sum = jnp.where(lane < POOL_GW, s2_scr[HALO:n_ext, :],
                        jnp.where(lane < 2 * POOL_GW, s4_scr[HALO:n_ext, :],
                                  jnp.where(lane < 3 * POOL_GW, s8_scr[HALO:n_ext, :], s16)))
    width = jnp.where(lane < POOL_GW, POOL_WINDOWS[0],
                      jnp.where(lane < 2 * POOL_GW, POOL_WINDOWS[1],
                                jnp.where(lane < 3 * POOL_GW, POOL_WINDOWS[2], POOL_WINDOWS[3])))
    cnt = jnp.minimum(width, i * tm + row + 1).astype(F32)
    pooled = win_sum / cnt - a
    ya_ref[0] = (_dot(pooled.astype(BF16), poolw_ref[...]) * pools_ref[...]).astype(BF16)

    lo_half = (lane % QK_DIM) < (QK_DIM // 2)

    def qk_norm_rope(t, g_ref, scale):
        ss = _dot((t * t).astype(BF16), bd32_ref[...])
        tn = t * lax.rsqrt(ss * (1.0 / QK_DIM) + EPS) * g_ref[...]
        rot = jnp.where(lo_half, pltpu.roll(tn, BR - QK_DIM // 2, 1), pltpu.roll(tn, QK_DIM // 2, 1))
        return (tn * cos_ref[...] + rot * sin_ref[...]) * scale

    qq = qk_norm_rope(proj[:, BR:2 * BR], qg_ref, QK_DIM ** -0.5 * LOG2E).astype(BF16)
    kk = qk_norm_rope(proj[:, 2 * BR:3 * BR], kg_ref, 1.0).astype(BF16)
    for g in range(QK_GROUPS):
        q_ref[0, g] = qq[:, g * QK_DIM:(g + 1) * QK_DIM]
        k_ref[0, g] = kk[:, g * QK_DIM:(g + 1) * QK_DIM]
    vt_ref[0] = proj[:, 3 * BR:4 * BR].T.astype(BF16)

    u_scr[0:HALO, :] = pc_h[:, 0:BR] * _sigmoid(pc_h[:, BR:2 * BR])
    u_scr[HALO:HALO + tm, :] = proj[:, 4 * BR:5 * BR] * _sigmoid(proj[:, 5 * BR:6 * BR])
    taps = [convw_ref[j:j + 1, :] for j in range(CONV_W)]
    convb = convb_ref[...]
    clng = clng_ref[...]
    clnb = clnb_ref[...]
    rc = 64
    base = HALO - (CONV_W - 1)
    for c in range(tm // rc):
        acc = convb + taps[0] * u_scr[c * rc + base:c * rc + base + rc, :]
        for j in range(1, CONV_W):
            acc = acc + taps[j] * u_scr[c * rc + base + j:c * rc + base + j + rc, :]
        y = _ln(acc, clng, clnb)
        yc_ref[0, c * rc:(c + 1) * rc, :] = (y * _sigmoid(y)).astype(BF16)

    d_in = proj[:, 6 * BR:8 * BR]
    z = 0.5 * d_in * (1.0 + lax.erf(d_in * (2.0 ** -0.5)))
    vv = _ln(z[:, BR:2 * BR], slng_ref[...], slnb_ref[...]).astype(BF16)
    tr = lax.broadcasted_iota(jnp.int32, (SGU_LEN, SGU_LEN), 0)
    tc = lax.broadcasted_iota(jnp.int32, (SGU_LEN, SGU_LEN), 1)
    wtri = [jnp.where(tr >= tc, sguw_ref[g], 0.0).astype(BF16) for g in range(SGU_GROUPS)]
    lane_g = lax.broadcasted_iota(jnp.int32, (SGU_LEN, BR), 1) // (BR // SGU_GROUPS)
    sgub = sgub_ref[...]
    for n in range(tm // SGU_LEN):
        vn = vv[n * SGU_LEN:(n + 1) * SGU_LEN, :]
        sv = _dot(wtri[0], vn)
        for g in range(1, SGU_GROUPS):
            sv = jnp.where(lane_g == g, _dot(wtri[g], vn), sv)
        yd_ref[0, n * SGU_LEN:(n + 1) * SGU_LEN, :] = (
            z[n * SGU_LEN:(n + 1) * SGU_LEN, 0:BR] * (sv + sgub)).astype(BF16)


def _mixer_in(x3, g1, win, cos_t, sin_t, qg, kg, bd32, poolw, pools, convw, convb, clng, clnb,
              slng, slnb, sguw, sgub, tm):
    B, S, D = x3.shape
    nt = S // tm
    hb = tm // HALO
    full = lambda *shape: pl.BlockSpec(shape, lambda b, i: (0,) * len(shape))
    in_specs = [
        pl.BlockSpec((1, tm, D), lambda b, i: (b, i, 0)),
        pl.BlockSpec((1, HALO, D), lambda b, i: (b, jnp.maximum(i * hb - 1, 0), 0)),
        full(1, D), full(D, 8 * BR),
        pl.BlockSpec((tm, BR), lambda b, i: (i, 0)),
        pl.BlockSpec((tm, BR), lambda b, i: (i, 0)),
        full(1, BR), full(1, BR), full(BR, BR),
        full(BR, BR), full(1, BR), full(CONV_W, BR), full(1, BR), full(1, BR), full(1, BR),
        full(1, BR), full(1, BR), full(SGU_GROUPS, SGU_LEN, SGU_LEN), full(SGU_LEN, BR),
    ]
    tok = lambda: pl.BlockSpec((1, tm, BR), lambda b, i: (b, i, 0))
    split = lambda: pl.BlockSpec((1, QK_GROUPS, tm, QK_DIM), lambda b, i: (b, 0, i, 0))
    out_specs = [
        split(), split(),
        pl.BlockSpec((1, BR, tm), lambda b, i: (b, 0, i)),
        tok(), tok(), tok(),
    ]
    out_shape = [
        jax.ShapeDtypeStruct((B, QK_GROUPS, S, QK_DIM), BF16),
        jax.ShapeDtypeStruct((B, QK_GROUPS, S, QK_DIM), BF16),
        jax.ShapeDtypeStruct((B, BR, S), BF16),
        jax.ShapeDtypeStruct((B, S, BR), BF16),
        jax.ShapeDtypeStruct((B, S, BR), BF16),
        jax.ShapeDtypeStruct((B, S, BR), BF16),
    ]
    ext = pltpu.VMEM((tm + HALO, BR), F32)
    return pl.pallas_call(
        _mixer_in_kernel,
        grid=(B, nt),
        in_specs=in_specs,
        out_specs=out_specs,
        out_shape=out_shape,
        scratch_shapes=[ext, ext, ext, ext, ext],
        compiler_params=pltpu.CompilerParams(
            dimension_semantics=("parallel", "parallel"), vmem_limit_bytes=V7X_VMEM_LIMIT),
        name="mixer_in",
    )(x3, x3, g1, win, cos_t, sin_t, qg, kg, bd32, poolw, pools, convw, convb, clng, clnb,
      slng, slnb, sguw, sgub)


def _attn_kernel(lam_ref, q_ref, k_ref, vt_ref, sg_ref, o_ref, p_scr, *, lam_init, tq):
    i = pl.program_id(1)
    key_chunk = lax.broadcasted_iota(jnp.int32, (tq, tq), 0) // CHUNK
    qry_chunk = lax.broadcasted_iota(jnp.int32, (tq, tq), 1) // CHUNK
    diag_mask = key_chunk <= qry_chunk

    def scores(g, j, masked):
        k0 = pl.multiple_of(j * tq, tq)
        st = lax.dot_general(k_ref[0, g, pl.ds(k0, tq), :], q_ref[0, g], (((1,), (1,)), ((), ())),
                             preferred_element_type=F32)
        return jnp.where(diag_mask, st, NEG_INF) if masked else st

    def fold8(x, op):
        return op(x.reshape(tq // 8, 8, tq), axis=0)

    def max_tile(j, ms, masked):
        return tuple(jnp.maximum(ms[g], fold8(scores(g, j, masked), jnp.max)) for g in range(QK_GROUPS))

    ms = tuple(jnp.full((8, tq), NEG_INF, F32) for _ in range(QK_GROUPS))
    ms = lax.fori_loop(0, i, functools.partial(max_tile, masked=False), ms)
    ms = max_tile(i, ms, True)
    m = [jnp.max(x, axis=0, keepdims=True) for x in ms]

    def acc_tile(j, carry, masked):
        k0 = pl.multiple_of(j * tq, tq)
        ls = []
        for g in range(QK_GROUPS):
            p = jnp.exp2(scores(g, j, masked) - m[g])
            p_scr[g] = p.astype(BF16)
            ls.append(carry[g][0] + fold8(p, jnp.sum))
        out = []
        for g in range(QK_GROUPS):
            hd = g // 2
            vt = vt_ref[0, hd * V_DIM:(hd + 1) * V_DIM, pl.ds(k0, tq)]
            out.append((ls[g], carry[g][1] + _dot(vt, p_scr[g])))
        return tuple(out)

    init = tuple((jnp.zeros((8, tq), F32), jnp.zeros((V_DIM, tq), F32)) for _ in range(QK_GROUPS))
    carry = lax.fori_loop(0, i, functools.partial(acc_tile, masked=False), init)
    carry = acc_tile(i, carry, True)
    carry = tuple((m[g], jnp.sum(carry[g][0], axis=0, keepdims=True), carry[g][1]) for g in range(QK_GROUPS))

    lamv = lam_ref[...]
    lam = (jnp.exp(jnp.sum(lamv[0:1] * lamv[1:2], axis=-1, keepdims=True))
           - jnp.exp(jnp.sum(lamv[2:3] * lamv[3:4], axis=-1, keepdims=True)) + lam_init)
    sg = sg_ref[...]
    heads = []
    for hd in range(HEADS):
        o0 = carry[2 * hd][2] / carry[2 * hd][1]
        o1 = carry[2 * hd + 1][2] / carry[2 * hd + 1][1]
        o = o0 - lam * o1
        o = o * lax.rsqrt(jnp.mean(o * o, axis=0, keepdims=True) + EPS) * sg * (1.0 - lam_init)
        heads.append(o)
    o_ref[0] = jnp.concatenate(heads, axis=0).T.astype(BF16)


def _attention(lamv, q, k, vt, sg_col, lam_init, tq):
    B, _, S, _ = q.shape
    return pl.pallas_call(
        functools.partial(_attn_kernel, lam_init=lam_init, tq=tq),
        grid=(B, S // tq),
        in_specs=[
            pl.BlockSpec((4, QK_DIM), lambda b, i: (0, 0)),
            pl.BlockSpec((1, QK_GROUPS, tq, QK_DIM), lambda b, i: (b, 0, i, 0)),
            pl.BlockSpec((1, QK_GROUPS, S, QK_DIM), lambda b, i: (b, 0, 0, 0)),
            pl.BlockSpec((1, BR, S), lambda b, i: (b, 0, 0)),
            pl.BlockSpec((V_DIM, tq), lambda b, i: (0, 0)),
        ],
        out_specs=pl.BlockSpec((1, tq, BR), lambda b, i: (b, i, 0)),
        out_shape=jax.ShapeDtypeStruct((B, S, BR), BF16),
        scratch_shapes=[pltpu.VMEM((QK_GROUPS, tq, tq), BF16)],
        compiler_params=pltpu.CompilerParams(
            dimension_semantics=("parallel", "parallel"), vmem_limit_bytes=V7X_VMEM_LIMIT),
        name="diff_attn",
    )(lamv, q, k, vt, sg_col)


def _merge_kernel(*refs, with_router):
    if with_router:
        (x_ref, ya_ref, yb_ref, yc_ref, yd_ref, g1_ref, wg_ref, bg_ref, wb_ref, wo_ref, g2_ref, rt_ref,
         x1_ref, h2_ref, gates_ref) = refs
    else:
        (x_ref, ya_ref, yb_ref, yc_ref, yd_ref, g1_ref, wg_ref, bg_ref, wb_ref, wo_ref, g2_ref,
         x1_ref, h2_ref) = refs
    x = x_ref[...]
    d = x.shape[1]
    h = _rms(x, g1_ref[...]).astype(BF16)
    merged = None
    for b, y_ref in enumerate((ya_ref, yb_ref, yc_ref, yd_ref)):
        gate = _sigmoid(_dot(h, wg_ref[:, b * d:(b + 1) * d]) + bg_ref[:, b * d:(b + 1) * d])
        term = gate * _dot(y_ref[...], wb_ref[b])
        merged = term if merged is None else merged + term
    x1 = x + _dot(merged.astype(BF16), wo_ref[...])
    x1_ref[...] = x1
    h2 = _rms(x1, g2_ref[...])
    h2_ref[...] = h2.astype(BF16)
    if with_router:
        logits = lax.dot_general(rt_ref[...], h2, (((1,), (1,)), ((), ())),
                                 preferred_element_type=F32, precision=lax.Precision.HIGHEST)
        e_idx = lax.broadcasted_iota(jnp.int32, logits.shape, 0)
        v1 = jnp.max(logits, axis=0, keepdims=True)
        i1 = jnp.min(jnp.where(logits == v1, e_idx, N_EXPERTS), axis=0, keepdims=True)
        rest = jnp.where(e_idx == i1, -jnp.inf, logits)
        v2 = jnp.max(rest, axis=0, keepdims=True)
        i2 = jnp.min(jnp.where(rest == v2, e_idx, N_EXPERTS), axis=0, keepdims=True)
        w2 = 1.0 / (1.0 + jnp.exp(v1 - v2))
        gates_ref[...] = jnp.where(e_idx == i1, 1.0 - w2, jnp.where(e_idx == i2, w2, 0.0))


def _merge(x2, ya, yb, yc, yd, g1, wg, bg, wb, wo, g2, router_t, tm):
    T, D = x2.shape
    with_router = router_t is not None
    full = lambda *shape: pl.BlockSpec(shape, lambda i: (0,) * len(shape))
    tokb = lambda w: pl.BlockSpec((tm, w), lambda i: (i, 0))
    in_specs = [tokb(D), tokb(BR), tokb(BR), tokb(BR), tokb(BR),
                full(1, D), full(D, N_BRANCH * D), full(1, N_BRANCH * D), full(N_BRANCH, BR, D),
                full(D, D), full(1, D)]
    args = [x2, ya, yb, yc, yd, g1, wg, bg, wb, wo, g2]
    out_specs = [tokb(D), tokb(D)]
    out_shape = [jax.ShapeDtypeStruct((T, D), F32), jax.ShapeDtypeStruct((T, D), BF16)]
    if with_router:
        in_specs.append(full(N_EXPERTS, D))
        args.append(router_t)
        out_specs.append(pl.BlockSpec((N_EXPERTS, tm), lambda i: (0, i)))
        out_shape.append(jax.ShapeDtypeStruct((N_EXPERTS, T), F32))
    return pl.pallas_call(
        functools.partial(_merge_kernel, with_router=with_router),
        grid=(T // tm,),
        in_specs=in_specs,
        out_specs=out_specs,
        out_shape=out_shape,
        compiler_params=pltpu.CompilerParams(
            dimension_semantics=("parallel",), vmem_limit_bytes=V7X_VMEM_LIMIT),
        name="merge_router" if with_router else "merge",
    )(*args)


def _ffn_kernel(x1_ref, h2_ref, w1_ref, w3_ref, w2_ref, o_ref):
    @pl.when(pl.program_id(1) == 0)
    def _():
        o_ref[...] = x1_ref[...]

    h2 = h2_ref[...]
    a = _dot(h2, w1_ref[...])
    mid = a * _sigmoid(a) * _dot(h2, w3_ref[...])
    o_ref[...] += _dot(mid.astype(BF16), w2_ref[...])


def _ffn(x1, h2, w1, w3, w2, tm, tf):
    T, D = x1.shape
    F = w1.shape[1]
    return pl.pallas_call(
        _ffn_kernel,
        grid=(T // tm, F // tf),
        in_specs=[pl.BlockSpec((tm, D), lambda i, f: (i, 0)),
                  pl.BlockSpec((tm, D), lambda i, f: (i, 0)),
                  pl.BlockSpec((D, tf), lambda i, f: (0, f)),
                  pl.BlockSpec((D, tf), lambda i, f: (0, f)),
                  pl.BlockSpec((tf, D), lambda i, f: (f, 0))],
        out_specs=pl.BlockSpec((tm, D), lambda i, f: (i, 0)),
        out_shape=jax.ShapeDtypeStruct((T, D), F32),
        compiler_params=pltpu.CompilerParams(
            dimension_semantics=("parallel", "arbitrary"), vmem_limit_bytes=V7X_VMEM_LIMIT),
        name="dense_ffn",
    )(x1, h2, w1, w3, w2)


ROW_ALIGN = 16
META_LANES = 128


def _moe_dims(T, tc, tm):
    n_ct = T // tc
    local_rows = 2 * tc + N_EXPERTS * ROW_ALIGN
    max_rows = 2 * T + n_ct * N_EXPERTS * (ROW_ALIGN - 1)
    n_mt = -(-max_rows // tm) + N_EXPERTS
    assert n_ct <= META_LANES and n_mt <= META_LANES and tm & (tm - 1) == 0
    return n_ct, local_rows, n_mt


def _sublane_excl_scan(x):
    rows, run = [], jnp.zeros_like(x[0:1, :])
    for e in range(x.shape[0]):
        rows.append(run)
        run = run + x[e:e + 1, :]
    return jnp.concatenate(rows, axis=0)


def _plan_kernel(gates_ref, member_ref, tri_ref, goff_ref, loc_ref, nch_ref, meta_ref, *, tm):
    routed = (gates_ref[...] > 0.0).astype(BF16)
    cnt = _dot(routed, member_ref[...]).astype(jnp.int32)
    row_sh, tm_sh = ROW_ALIGN.bit_length() - 1, tm.bit_length() - 1
    pad = ((cnt + (ROW_ALIGN - 1)) >> row_sh) << row_sh
    incl = jnp.dot(pad.astype(F32), tri_ref[...], preferred_element_type=F32,
                   precision=lax.Precision.HIGHEST).astype(jnp.int32)
    tot = jnp.max(incl, axis=1, keepdims=True)
    etot = ((tot + (tm - 1)) >> tm_sh) << tm_sh
    base = _sublane_excl_scan(etot)
    goff_ref[...] = base + incl - pad
    loc_ref[...] = _sublane_excl_scan(pad)
    nch_ref[...] = pad >> row_sh

    lane = lax.broadcasted_iota(jnp.int32, (N_EXPERTS, META_LANES), 1)
    sub = lax.broadcasted_iota(jnp.int32, (N_EXPERTS, META_LANES), 0)
    end = base + etot
    n_used = jnp.max(end, axis=0, keepdims=True) >> tm_sh
    src = jnp.minimum(lane, n_used - 1)
    expert = jnp.minimum(jnp.sum((src * tm >= end).astype(jnp.int32), axis=0, keepdims=True), N_EXPERTS - 1)
    start = lane * tm
    inside = (start >= base) & (start < end)
    valid = jnp.sum(jnp.where(inside, jnp.clip(tot - (start - base), 0, tm), 0), axis=0, keepdims=True)
    fill_start = jnp.sum(jnp.where(sub == lane, base + tot, 0), axis=0, keepdims=True)
    fill_nch = jnp.sum(jnp.where(sub == lane, (etot - tot) >> row_sh, 0), axis=0, keepdims=True)
    meta = jnp.where(sub == 0, expert, jnp.where(sub == 1, valid, jnp.where(sub == 2, src,
                     jnp.where(sub == 3, fill_start, jnp.where(sub == 4, fill_nch,
                     jnp.where(sub == 5, n_used, 0))))))
    meta_ref[...] = meta


def _moe_plan(gates_t, tc, tm):
    E, T = gates_t.shape
    member = (jnp.arange(T)[:, None] // tc == jnp.arange(META_LANES)[None, :]).astype(BF16)
    tri = (jnp.arange(META_LANES)[:, None] <= jnp.arange(META_LANES)[None, :]).astype(F32)
    tbl = jax.ShapeDtypeStruct((N_EXPERTS, META_LANES), jnp.int32)
    return pl.pallas_call(
        functools.partial(_plan_kernel, tm=tm),
        out_shape=[tbl, tbl, tbl, tbl],
        compiler_params=pltpu.CompilerParams(vmem_limit_bytes=V7X_VMEM_LIMIT),
        name="moe_plan",
    )(gates_t, member, tri)


def _segment_rows(gates, loc_s, tile, strict_upper):
    routed = gates > 0.0
    pos = _dot(routed.astype(BF16), strict_upper).astype(jnp.int32)
    sub = lax.broadcasted_iota(jnp.int32, gates.shape, 0)
    loc = jnp.zeros(gates.shape, jnp.int32)
    for e in range(N_EXPERTS):
        loc = jnp.where(sub == e, loc_s[e, tile], loc)
    rows = loc + pos
    row_a = jnp.min(jnp.where(routed, rows, jnp.iinfo(jnp.int32).max), axis=0, keepdims=True)
    row_b = jnp.max(jnp.where(routed, rows, -1), axis=0, keepdims=True)
    gate_a = jnp.sum(jnp.where(routed & (rows == row_a), gates, 0.0), axis=0, keepdims=True)
    gate_b = jnp.sum(jnp.where(routed & (rows == row_b), gates, 0.0), axis=0, keepdims=True)
    return row_a, row_b, gate_a, gate_b


def _chunk_copy(src, dst, src_row, dst_row, sem):
    aligned = lambda r: r if isinstance(r, int) else pl.multiple_of(r, ROW_ALIGN)
    return pltpu.make_async_copy(src.at[pl.ds(aligned(src_row), ROW_ALIGN)],
                                 dst.at[pl.ds(aligned(dst_row), ROW_ALIGN)], sem)


def _dispatch_kernel(goff_s, loc_s, nch_s, meta_s, gates_ref, h2_ref, upper_ref, xs_hbm,
                     xc_scr, zero_scr, sem, tile_sem):
    i = pl.program_id(0)
    n_ct = pl.num_programs(0)
    row_a, row_b, _, _ = _segment_rows(gates_ref[...], loc_s, i, upper_ref[...])
    r_iota = lax.broadcasted_iota(jnp.int32, (xc_scr.shape[0], gates_ref.shape[1]), 0)
    onehot = jnp.where(r_iota == row_a, 1.0, jnp.where(r_iota == row_b, 1.0, 0.0)).astype(BF16)
    xc_scr[...] = _dot(onehot, h2_ref[...]).astype(BF16)

    total = 0
    for e in range(N_EXPERTS):
        def issue(c, carry, e=e):
            _chunk_copy(xc_scr, xs_hbm, loc_s[e, i] + c * ROW_ALIGN, goff_s[e, i] + c * ROW_ALIGN, sem).start()
            return carry
        lax.fori_loop(0, nch_s[e, i], issue, 0)
        total = total + nch_s[e, i]

    def drain(c, carry):
        _chunk_copy(xc_scr, xs_hbm, 0, 0, sem).wait()
        return carry
    lax.fori_loop(0, total, drain, 0)

    @pl.when(i == n_ct - 1)
    def _():
        zero_scr[...] = jnp.zeros(zero_scr.shape, zero_scr.dtype)
        fills = 0
        for e in range(N_EXPERTS):
            def fill(c, carry, e=e):
                _chunk_copy(zero_scr, xs_hbm, 0, meta_s[3, e] + c * ROW_ALIGN, sem).start()
                return carry
            lax.fori_loop(0, meta_s[4, e], fill, 0)
            fills = fills + meta_s[4, e]

        def drain_fill(c, carry):
            _chunk_copy(zero_scr, xs_hbm, 0, 0, sem).wait()
            return carry
        lax.fori_loop(0, fills, drain_fill, 0)

        tm = zero_scr.shape[0]
        n_mt = xs_hbm.shape[0] // tm

        def tile_copy(t):
            return pltpu.make_async_copy(zero_scr, xs_hbm.at[pl.ds(pl.multiple_of(t * tm, tm), tm)], tile_sem)

        def fill_tile(t, carry):
            tile_copy(t).start()
            return carry
        lax.fori_loop(meta_s[5, 0], n_mt, fill_tile, 0)

        def drain_tile(t, carry):
            tile_copy(t).wait()
            return carry
        lax.fori_loop(meta_s[5, 0], n_mt, drain_tile, 0)


def _moe_dispatch(tables, gates_t, h2, tc, tm):
    T, D = h2.shape
    n_ct, local_rows, n_mt = _moe_dims(T, tc, tm)
    upper = (jnp.arange(tc)[:, None] < jnp.arange(tc)[None, :]).astype(BF16)
    return pl.pallas_call(
        _dispatch_kernel,
        grid_spec=pltpu.PrefetchScalarGridSpec(
            num_scalar_prefetch=4,
            grid=(n_ct,),
            in_specs=[pl.BlockSpec((N_EXPERTS, tc), lambda i, *_: (0, i)),
                      pl.BlockSpec((tc, D), lambda i, *_: (i, 0)),
                      pl.BlockSpec((tc, tc), lambda i, *_: (0, 0))],
            out_specs=pl.BlockSpec(memory_space=pl.ANY),
            scratch_shapes=[pltpu.VMEM((local_rows, D), BF16), pltpu.VMEM((tm, D), BF16),
                            pltpu.SemaphoreType.DMA(()), pltpu.SemaphoreType.DMA(())]),
        out_shape=jax.ShapeDtypeStruct((n_mt * tm, D), BF16),
        compiler_params=pltpu.CompilerParams(
            dimension_semantics=("arbitrary",), vmem_limit_bytes=V7X_VMEM_LIMIT),
        name="moe_dispatch",
    )(*tables, gates_t, h2, upper)


def _experts_kernel(meta_s, xs_ref, w1_ref, w3_ref, w2_ref, ys_ref, acc_scr):
    m = pl.program_id(0)
    f = pl.program_id(1)

    @pl.when(meta_s[1, m] > 0)
    def _():
        x = xs_ref[...]
        a = _dot(x, w1_ref[0])
        mid = a * _sigmoid(a) * _dot(x, w3_ref[0])
        part = _dot(mid.astype(BF16), w2_ref[0])

        @pl.when(f == 0)
        def _():
            acc_scr[...] = part

        @pl.when(f > 0)
        def _():
            acc_scr[...] += part

        @pl.when(f == pl.num_programs(1) - 1)
        def _():
            ys_ref[...] = acc_scr[...].astype(BF16)

    @pl.when((meta_s[1, m] == 0) & (f == pl.num_programs(1) - 1))
    def _():
        ys_ref[...] = jnp.zeros(ys_ref.shape, ys_ref.dtype)


def _moe_experts(meta, xs, w1, w3, w2, tm, tf):
    rows, D = xs.shape
    F = w1.shape[2]
    nf = F // tf

    def f_eff(m, f, meta):
        return jnp.where(meta[1, m] > 0, f, nf - 1)

    return pl.pallas_call(
        _experts_kernel,
        grid_spec=pltpu.PrefetchScalarGridSpec(
            num_scalar_prefetch=1,
            grid=(rows // tm, nf),
            in_specs=[pl.BlockSpec((tm, D), lambda m, f, meta: (meta[2, m], 0)),
                      pl.BlockSpec((1, D, tf), lambda m, f, meta: (meta[0, m], 0, f_eff(m, f, meta))),
                      pl.BlockSpec((1, D, tf), lambda m, f, meta: (meta[0, m], 0, f_eff(m, f, meta))),
                      pl.BlockSpec((1, tf, D), lambda m, f, meta: (meta[0, m], f_eff(m, f, meta), 0))],
            out_specs=pl.BlockSpec((tm, D), lambda m, f, meta: (m, 0)),
            scratch_shapes=[pltpu.VMEM((tm, D), F32)]),
        out_shape=jax.ShapeDtypeStruct((rows, D), BF16),
        compiler_params=pltpu.CompilerParams(
            dimension_semantics=("arbitrary", "arbitrary"), vmem_limit_bytes=V7X_VMEM_LIMIT),
        name="moe_experts",
    )(meta, xs, w1, w3, w2)


def _combine_kernel(goff_s, loc_s, nch_s, gates_ref, x1_ref, upper_ref, ys_hbm, o_ref, yc_scr, sem):
    i = pl.program_id(0)
    tc = gates_ref.shape[1]

    @pl.when(i == 0)
    def _():
        yc_scr[...] = jnp.zeros(yc_scr.shape, yc_scr.dtype)

    total = 0
    for e in range(N_EXPERTS):
        def issue(c, carry, e=e):
            _chunk_copy(ys_hbm, yc_scr, goff_s[e, i] + c * ROW_ALIGN, loc_s[e, i] + c * ROW_ALIGN, sem).start()
            return carry
        lax.fori_loop(0, nch_s[e, i], issue, 0)
        total = total + nch_s[e, i]

    row_a, row_b, gate_a, gate_b = _segment_rows(gates_ref[...], loc_s, i, upper_ref[...])
    stats = jnp.concatenate([row_a.astype(F32), row_b.astype(F32), gate_a, gate_b,
                             jnp.zeros((META_LANES - 4, tc), F32)], axis=0).T
    lane = lax.broadcasted_iota(jnp.int32, (tc, yc_scr.shape[0]), 1).astype(F32)
    sel = jnp.where(lane == stats[:, 0:1], stats[:, 2:3],
                    jnp.where(lane == stats[:, 1:2], stats[:, 3:4], 0.0)).astype(BF16)

    def drain(c, carry):
        _chunk_copy(ys_hbm, yc_scr, 0, 0, sem).wait()
        return carry
    lax.fori_loop(0, total, drain, 0)
    o_ref[...] = x1_ref[...] + _dot(sel, yc_scr[...])


def _moe_combine(tables, gates_t, x1, ys, tc, tm):
    T, D = x1.shape
    n_ct, local_rows, _ = _moe_dims(T, tc, tm)
    upper = (jnp.arange(tc)[:, None] < jnp.arange(tc)[None, :]).astype(BF16)
    return pl.pallas_call(
        _combine_kernel,
        grid_spec=pltpu.PrefetchScalarGridSpec(
            num_scalar_prefetch=3,
            grid=(n_ct,),
            in_specs=[pl.BlockSpec((N_EXPERTS, tc), lambda i, *_: (0, i)),
                      pl.BlockSpec((tc, D), lambda i, *_: (i, 0)),
                      pl.BlockSpec((tc, tc), lambda i, *_: (0, 0)),
                      pl.BlockSpec(memory_space=pl.ANY)],
            out_specs=pl.BlockSpec((tc, D), lambda i, *_: (i, 0)),
            scratch_shapes=[pltpu.VMEM((local_rows, D), BF16), pltpu.SemaphoreType.DMA(())]),
        out_shape=jax.ShapeDtypeStruct((T, D), F32),
        compiler_params=pltpu.CompilerParams(
            dimension_semantics=("arbitrary",), vmem_limit_bytes=V7X_VMEM_LIMIT),
        name="moe_combine",
    )(*tables[:3], gates_t, x1, upper, ys)


def _moe(x1, h2, gates_t, w1, w3, w2, tc, tm, tf):
    goff, loc, nch, meta = _moe_plan(gates_t, tc, tm)
    xs = _moe_dispatch((goff, loc, nch, meta), gates_t, h2, tc, tm)
    ys = _moe_experts(meta, xs, w1, w3, w2, tm, tf)
    return _moe_combine((goff, loc, nch), gates_t, x1, ys, tc, tm)


def _rope_tables(seq):
    pos = jnp.arange(seq, dtype=F32)
    inv = 1.0 / (ROPE_THETA ** (jnp.arange(0, QK_DIM, 2, dtype=F32) / QK_DIM))
    ang = pos[:, None] * inv[None, :]
    cos, sin = jnp.cos(ang), jnp.sin(ang)
    groups = BR // QK_DIM
    cos_t = jnp.tile(jnp.concatenate([cos, cos], axis=-1), (1, groups))
    sin_t = jnp.tile(jnp.concatenate([-sin, sin], axis=-1), (1, groups))
    return cos_t, sin_t


def _block_diag(blocks):
    g, n, m = blocks.shape
    eye = jnp.eye(g, dtype=blocks.dtype)
    return (eye[:, None, :, None] * blocks[:, :, None, :]).reshape(g * n, g * m)


def kernel(x, norm1_g, w_in, w_gate, b_gate, w_branch, w_out, pool_w, pool_scale, attn_qn_g, attn_kn_g,
           lam_q1, lam_k1, lam_q2, lam_k2, attn_subln_g, conv_w, conv_b, conv_ln_g, conv_ln_b,
           sgu_ln_g, sgu_ln_b, sgu_w, sgu_b, norm2_g, ffn_w1, ffn_w3, ffn_w2, moe_router,
           moe_w1, moe_w3, moe_w2):
    B, S, D = x.shape
    depth = w_in.shape[0]
    T = B * S
    tm_mix = min(512, S)
    tq = min(256, S)
    tm_merge = min(512, T)
    tm_ffn = min(1024, T)
    tm_moe = 512

    cos_t, sin_t = _rope_tables(S)
    bd32 = _block_diag(jnp.ones((BR // QK_DIM, QK_DIM, QK_DIM), BF16))
    row = lambda v: v.reshape(1, -1).astype(F32)

    for l in range(depth):
        x3 = x.reshape(B, S, D)
        q, k, vt, ya, yc, yd = _mixer_in(
            x3, row(norm1_g[l]), w_in[l].astype(BF16), cos_t, sin_t,
            row(jnp.tile(attn_qn_g[l], BR // QK_DIM)), row(jnp.tile(attn_kn_g[l], BR // QK_DIM)), bd32,
            _block_diag(pool_w[l]).astype(BF16), row(pool_scale[l]),
            conv_w[l], row(conv_b[l]), row(conv_ln_g[l]), row(conv_ln_b[l]),
            row(sgu_ln_g[l]), row(sgu_ln_b[l]), sgu_w[l],
            jnp.repeat(sgu_b[l].T, BR // SGU_GROUPS, axis=1), tm_mix)
        lamv = jnp.stack([lam_q1[l], lam_k1[l], lam_q2[l], lam_k2[l]]).astype(F32)
        lam_init = 0.8 - 0.6 * math.exp(-0.3 * l)
        sg_col = jnp.broadcast_to(attn_subln_g[l].astype(F32)[:, None], (V_DIM, tq))
        yb = _attention(lamv, q, k, vt, sg_col, lam_init, tq)

        is_moe = l % 2 == 1
        router_t = moe_router[l // 2].T.astype(F32) if is_moe else None
        flat = lambda y: y.reshape(T, BR)
        outs = _merge(x.reshape(T, D), flat(ya), flat(yb), flat(yc), flat(yd), row(norm1_g[l]),
                      w_gate[l].astype(BF16), row(b_gate[l]), w_branch[l].astype(BF16),
                      w_out[l].astype(BF16), row(norm2_g[l]), router_t, tm_merge)
        if is_moe:
            x1, h2, gates_t = outs
            i = l // 2
            x = _moe(x1, h2, gates_t, moe_w1[i].astype(BF16), moe_w3[i].astype(BF16), moe_w2[i].astype(BF16),
                     tm_merge, tm_moe, 512)
        else:
            x1, h2 = outs
            i = l // 2
            x = _ffn(x1, h2, ffn_w1[i].astype(BF16), ffn_w3[i].astype(BF16), ffn_w2[i].astype(BF16),
                     tm_ffn, ffn_w1.shape[2] // 2)
    return x.reshape(B, S, D)
```

```python
import functools
import math

import jax
import jax.numpy as jnp
from jax import lax
from jax.experimental import pallas as pl
from jax.experimental.pallas import tpu as pltpu

F32 = jnp.float32
BF16 = jnp.bfloat16

N_BRANCH = 4
BR = 256
POOL_WINDOWS = (2, 4, 8, 16)
POOL_GW = 64
HEADS = 4
QK_DIM = 32
V_DIM = 64
ROPE_THETA = 10000.0
CHUNK = 64
CONV_W = 31
SGU_LEN = 128
SGU_GROUPS = 4
N_EXPERTS = 8
EPS = 1e-6
QK_GROUPS = 2 * HEADS
LOG2E = math.log2(math.e)
SAFE_SCORE_BOUND = 40.0
NEG_INF = -1e30

HALO = 32
V7X_VMEM_LIMIT = 56 * 1024 * 1024


def _rms(x, g):
    return x * lax.rsqrt(jnp.mean(x * x, axis=-1, keepdims=True) + EPS) * g


def _ln(x, g, b):
    mu = jnp.mean(x, axis=-1, keepdims=True)
    xc = x - mu
    return xc * lax.rsqrt(jnp.mean(xc * xc, axis=-1, keepdims=True) + EPS) * g + b


def _sigmoid(x):
    return 1.0 / (1.0 + jnp.exp(-x))


def _dot(a, b):
    return jnp.dot(a, b, preferred_element_type=F32)


def _mixer_in_kernel(x_ref, xh_ref, g1_ref, win_ref, cos_ref, sin_ref, qg_ref, kg_ref, bd32_ref,
                     poolw_ref, pools_ref, convw_ref, convb_ref, clng_ref, clnb_ref,
                     slng_ref, slnb_ref, sguw_ref, sgub_ref,
                     q_ref, k_ref, vt_ref, ya_ref, yc_ref, yd_ref,
                     a_scr, s2_scr, s4_scr, s8_scr, u_scr, us_scr):
    i = pl.program_id(1)
    tm = x_ref.shape[1]
    g1 = g1_ref[...]
    h = _rms(x_ref[0], g1).astype(BF16)
    proj = _dot(h, win_ref[...])

    hh = _rms(xh_ref[0], g1).astype(BF16)
    halo_ok = (i > 0).astype(F32)
    pa_h = _dot(hh, win_ref[:, 0:BR]) * halo_ok
    pc_h = _dot(hh, win_ref[:, 4 * BR:6 * BR]) * halo_ok

    a = proj[:, 0:BR]
    a_scr[0:HALO, :] = pa_h
    a_scr[HALO:HALO + tm, :] = a
    n_ext = tm + HALO
    s2_scr[8:n_ext, :] = a_scr[8:n_ext, :] + a_scr[7:n_ext - 1, :]
    s4_scr[16:n_ext, :] = s2_scr[16:n_ext, :] + s2_scr[14:n_ext - 2, :]
    s8_scr[24:n_ext, :] = s4_scr[24:n_ext, :] + s4_scr[20:n_ext - 4, :]
    s16 = s8_scr[HALO:n_ext, :] + s8_scr[HALO - 8:n_ext - 8, :]
    lane = lax.broadcasted_iota(jnp.int32, (tm, BR), 1)
    row = lax.broadcasted_iota(jnp.int32, (tm, BR), 0)
    win_sum = jnp.where(lane < POOL_GW, s2_scr[HALO:n_ext, :],
                        jnp.where(lane < 2 * POOL_GW, s4_scr[HALO:n_ext, :],
                                  jnp.where(lane < 3 * POOL_GW, s8_scr[HALO:n_ext, :], s16)))
    width = jnp.where(lane < POOL_GW, POOL_WINDOWS[0],
                      jnp.where(lane < 2 * POOL_GW, POOL_WINDOWS[1],
                                jnp.where(lane < 3 * POOL_GW, POOL_WINDOWS[2], POOL_WINDOWS[3])))
    cnt = jnp.minimum(width, i * tm + row + 1).astype(F32)
    pooled = win_sum / cnt - a
    ya_ref[0] = (_dot(pooled.astype(BF16), poolw_ref[...]) * pools_ref[...]).astype(BF16)

    lo_half = (lane % QK_DIM) < (QK_DIM // 2)

    def qk_norm_rope(t, g_ref, scale):
        ss = _dot((t * t).astype(BF16), bd32_ref[...])
        tn = t * lax.rsqrt(ss * (1.0 / QK_DIM) + EPS) * g_ref[...]
        rot = jnp.where(lo_half, pltpu.roll(tn, BR - QK_DIM // 2, 1), pltpu.roll(tn, QK_DIM // 2, 1))
        return (tn * cos_ref[...] + rot * sin_ref[...]) * scale

    qq = qk_norm_rope(proj[:, BR:2 * BR], qg_ref, QK_DIM ** -0.5 * LOG2E).astype(BF16)
    kk = qk_norm_rope(proj[:, 2 * BR:3 * BR], kg_ref, 1.0).astype(BF16)
    for g in range(QK_GROUPS):
        q_ref[0, g] = qq[:, g * QK_DIM:(g + 1) * QK_DIM]
        k_ref[0, g] = kk[:, g * QK_DIM:(g + 1) * QK_DIM]
    vt_ref[0] = proj[:, 3 * BR:4 * BR].T.astype(BF16)

    u_scr[0:HALO, :] = pc_h[:, 0:BR] * _sigmoid(pc_h[:, BR:2 * BR])
    u_scr[HALO:HALO + tm, :] = proj[:, 4 * BR:5 * BR] * _sigmoid(proj[:, 5 * BR:6 * BR])
    taps = [convw_ref[j:j + 1, :] for j in range(CONV_W)]
    convb = convb_ref[...]
    clng = clng_ref[...]
    clnb = clnb_ref[...]
    n_sh = us_scr.shape[1]
    for s in range(1, 8):
        us_scr[s - 1] = u_scr[s:s + n_sh, :]
    rc = 64
    base = HALO - (CONV_W - 1)

    def tap_window(c, j):
        s, r0 = (base + j) % 8, c * rc + (base + j) // 8 * 8
        return u_scr[r0:r0 + rc, :] if s == 0 else us_scr[s - 1, r0:r0 + rc, :]

    for c in range(tm // rc):
        acc = convb + taps[0] * tap_window(c, 0)
        for j in range(1, CONV_W):
            acc = acc + taps[j] * tap_window(c, j)
        y = _ln(acc, clng, clnb)
        yc_ref[0, c * rc:(c + 1) * rc, :] = (y * _sigmoid(y)).astype(BF16)

    d_in = proj[:, 6 * BR:8 * BR]
    z = 0.5 * d_in * (1.0 + lax.erf(d_in * (2.0 ** -0.5)))
    vv = _ln(z[:, BR:2 * BR], slng_ref[...], slnb_ref[...]).astype(BF16)
    tr = lax.broadcasted_iota(jnp.int32, (SGU_LEN, SGU_LEN), 0)
    tc = lax.broadcasted_iota(jnp.int32, (SGU_LEN, SGU_LEN), 1)
    wtri = [jnp.where(tr >= tc, sguw_ref[g], 0.0).astype(BF16) for g in range(SGU_GROUPS)]
    lane_g = lax.broadcasted_iota(jnp.int32, (SGU_LEN, BR), 1) // (BR // SGU_GROUPS)
    sgub = sgub_ref[...]
    for n in range(tm // SGU_LEN):
        vn = vv[n * SGU_LEN:(n + 1) * SGU_LEN, :]
        sv = _dot(wtri[0], vn)
        for g in range(1, SGU_GROUPS):
            sv = jnp.where(lane_g == g, _dot(wtri[g], vn), sv)
        yd_ref[0, n * SGU_LEN:(n + 1) * SGU_LEN, :] = (
            z[n * SGU_LEN:(n + 1) * SGU_LEN, 0:BR] * (sv + sgub)).astype(BF16)


def _mixer_in(x3, g1, win, cos_t, sin_t, qg, kg, bd32, poolw, pools, convw, convb, clng, clnb,
              slng, slnb, sguw, sgub, tm):
    B, S, D = x3.shape
    nt = S // tm
    hb = tm // HALO
    full = lambda *shape: pl.BlockSpec(shape, lambda b, i: (0,) * len(shape))
    in_specs = [
        pl.BlockSpec((1, tm, D), lambda b, i: (b, i, 0)),
        pl.BlockSpec((1, HALO, D), lambda b, i: (b, jnp.maximum(i * hb - 1, 0), 0)),
        full(1, D), full(D, 8 * BR),
        pl.BlockSpec((tm, BR), lambda b, i: (i, 0)),
        pl.BlockSpec((tm, BR), lambda b, i: (i, 0)),
        full(1, BR), full(1, BR), full(BR, BR),
        full(BR, BR), full(1, BR), full(CONV_W, BR), full(1, BR), full(1, BR), full(1, BR),
        full(1, BR), full(1, BR), full(SGU_GROUPS, SGU_LEN, SGU_LEN), full(SGU_LEN, BR),
    ]
    tok = lambda: pl.BlockSpec((1, tm, BR), lambda b, i: (b, i, 0))
    split = lambda: pl.BlockSpec((1, QK_GROUPS, tm, QK_DIM), lambda b, i: (b, 0, i, 0))
    out_specs = [
        split(), split(),
        pl.BlockSpec((1, BR, tm), lambda b, i: (b, 0, i)),
        tok(), tok(), tok(),
    ]
    out_shape = [
        jax.ShapeDtypeStruct((B, QK_GROUPS, S, QK_DIM), BF16),
        jax.ShapeDtypeStruct((B, QK_GROUPS, S, QK_DIM), BF16),
        jax.ShapeDtypeStruct((B, BR, S), BF16),
        jax.ShapeDtypeStruct((B, S, BR), BF16),
        jax.ShapeDtypeStruct((B, S, BR), BF16),
        jax.ShapeDtypeStruct((B, S, BR), BF16),
    ]
    ext = pltpu.VMEM((tm + HALO, BR), F32)
    return pl.pallas_call(
        _mixer_in_kernel,
        grid=(B, nt),
        in_specs=in_specs,
        out_specs=out_specs,
        out_shape=out_shape,
        scratch_shapes=[ext, ext, ext, ext, ext, pltpu.VMEM((7, tm + HALO - 8, BR), F32)],
        compiler_params=pltpu.CompilerParams(
            dimension_semantics=("parallel", "parallel"), vmem_limit_bytes=V7X_VMEM_LIMIT),
        name="mixer_in",
    )(x3, x3, g1, win, cos_t, sin_t, qg, kg, bd32, poolw, pools, convw, convb, clng, clnb,
      slng, slnb, sguw, sgub)


def _attn_kernel(bounded_s, lam_ref, q_ref, k_ref, vt_ref, sg_ref, o_ref, p_scr, *, lam_init, tq):
    i = pl.program_id(1)
    key_chunk = lax.broadcasted_iota(jnp.int32, (tq, tq), 0) // CHUNK
    qry_chunk = lax.broadcasted_iota(jnp.int32, (tq, tq), 1) // CHUNK
    diag_mask = key_chunk <= qry_chunk

    def scores(g, j, masked):
        k0 = pl.multiple_of(j * tq, tq)
        st = lax.dot_general(k_ref[0, g, pl.ds(k0, tq), :], q_ref[0, g], (((1,), (1,)), ((), ())),
                             preferred_element_type=F32)
        return jnp.where(diag_mask, st, NEG_INF) if masked else st

    def fold8(x, op):
        return op(x.reshape(tq // 8, 8, tq), axis=0)

    def exact_maxima():
        def max_tile(j, ms, masked):
            return tuple(jnp.maximum(ms[g], fold8(scores(g, j, masked), jnp.max)) for g in range(QK_GROUPS))

        ms = tuple(jnp.full((8, tq), NEG_INF, F32) for _ in range(QK_GROUPS))
        ms = lax.fori_loop(0, i, functools.partial(max_tile, masked=False), ms)
        ms = max_tile(i, ms, True)
        return [jnp.max(x, axis=0, keepdims=True) for x in ms]

    def softmax_pv(shift):
        def acc_tile(j, carry, masked):
            k0 = pl.multiple_of(j * tq, tq)
            ls = []
            for g in range(QK_GROUPS):
                st = scores(g, j, masked)
                p = jnp.exp2(st if shift is None else st - shift[g])
                p_scr[g] = p.astype(BF16)
                ls.append(carry[g][0] + fold8(p, jnp.sum))
            out = []
            for g in range(QK_GROUPS):
                hd = g // 2
                vt = vt_ref[0, hd * V_DIM:(hd + 1) * V_DIM, pl.ds(k0, tq)]
                out.append((ls[g], carry[g][1] + _dot(vt, p_scr[g])))
            return tuple(out)

        init = tuple((jnp.zeros((8, tq), F32), jnp.zeros((V_DIM, tq), F32)) for _ in range(QK_GROUPS))
        carry = lax.fori_loop(0, i, functools.partial(acc_tile, masked=False), init)
        carry = acc_tile(i, carry, True)
        return [(jnp.sum(l8, axis=0, keepdims=True), acc) for l8, acc in carry]

    def finish(res):
        lamv = lam_ref[...]
        lam = (jnp.exp(jnp.sum(lamv[0:1] * lamv[1:2], axis=-1, keepdims=True))
               - jnp.exp(jnp.sum(lamv[2:3] * lamv[3:4], axis=-1, keepdims=True)) + lam_init)
        sg = sg_ref[...]
        heads = []
        for hd in range(HEADS):
            o = res[2 * hd][1] / res[2 * hd][0] - lam * (res[2 * hd + 1][1] / res[2 * hd + 1][0])
            o = o * lax.rsqrt(jnp.mean(o * o, axis=0, keepdims=True) + EPS) * sg * (1.0 - lam_init)
            heads.append(o)
        o_ref[0] = jnp.concatenate(heads, axis=0).T.astype(BF16)

    @pl.when(bounded_s[0] == 1)
    def _():
        finish(softmax_pv(None))

    @pl.when(bounded_s[0] != 1)
    def _():
        finish(softmax_pv(exact_maxima()))


def _attention(bounded, lamv, q, k, vt, sg_col, lam_init, tq):
    B, _, S, _ = q.shape
    return pl.pallas_call(
        functools.partial(_attn_kernel, lam_init=lam_init, tq=tq),
        grid_spec=pltpu.PrefetchScalarGridSpec(
            num_scalar_prefetch=1,
            grid=(B, S // tq),
            in_specs=[
                pl.BlockSpec((4, QK_DIM), lambda b, i, *_: (0, 0)),
                pl.BlockSpec((1, QK_GROUPS, tq, QK_DIM), lambda b, i, *_: (b, 0, i, 0)),
                pl.BlockSpec((1, QK_GROUPS, S, QK_DIM), lambda b, i, *_: (b, 0, 0, 0)),
                pl.BlockSpec((1, BR, S), lambda b, i, *_: (b, 0, 0)),
                pl.BlockSpec((V_DIM, tq), lambda b, i, *_: (0, 0)),
            ],
            out_specs=pl.BlockSpec((1, tq, BR), lambda b, i, *_: (b, i, 0)),
            scratch_shapes=[pltpu.VMEM((QK_GROUPS, tq, tq), BF16)]),
        out_shape=jax.ShapeDtypeStruct((B, S, BR), BF16),
        compiler_params=pltpu.CompilerParams(
            dimension_semantics=("parallel", "parallel"), vmem_limit_bytes=V7X_VMEM_LIMIT),
        name="diff_attn",
    )(bounded, lamv, q, k, vt, sg_col)


def _merge_kernel(*refs, with_router):
    if with_router:
        (x_ref, ya_ref, yb_ref, yc_ref, yd_ref, g1_ref, wg_ref, bg_ref, wb_ref, wo_ref, g2_ref, rt_ref,
         x1_ref, h2_ref, gates_ref) = refs
    else:
        (x_ref, ya_ref, yb_ref, yc_ref, yd_ref, g1_ref, wg_ref, bg_ref, wb_ref, wo_ref, g2_ref,
         x1_ref, h2_ref) = refs
    x = x_ref[...]
    d = x.shape[1]
    h = _rms(x, g1_ref[...]).astype(BF16)
    merged = None
    for b, y_ref in enumerate((ya_ref, yb_ref, yc_ref, yd_ref)):
        gate = _sigmoid(_dot(h, wg_ref[:, b * d:(b + 1) * d]) + bg_ref[:, b * d:(b + 1) * d])
        term = gate * _dot(y_ref[...], wb_ref[b])
        merged = term if merged is None else merged + term
    x1 = x + _dot(merged.astype(BF16), wo_ref[...])
    x1_ref[...] = x1
    h2 = _rms(x1, g2_ref[...])
    h2_ref[...] = h2.astype(BF16)
    if with_router:
        def split(t):
            hi = t.astype(BF16)
            return hi, (t - hi.astype(F32)).astype(BF16)

        def nt_dot(a, b):
            return lax.dot_general(a, b, (((1,), (1,)), ((), ())), preferred_element_type=F32)

        (r_hi, r_lo), (h_hi, h_lo) = split(rt_ref[...]), split(h2)
        logits = nt_dot(r_hi, h_hi) + (nt_dot(r_hi, h_lo) + nt_dot(r_lo, h_hi))
        e_idx = lax.broadcasted_iota(jnp.int32, logits.shape, 0)
        v1 = jnp.max(logits, axis=0, keepdims=True)
        i1 = jnp.min(jnp.where(logits == v1, e_idx, N_EXPERTS), axis=0, keepdims=True)
        rest = jnp.where(e_idx == i1, -jnp.inf, logits)
        v2 = jnp.max(rest, axis=0, keepdims=True)
        i2 = jnp.min(jnp.where(rest == v2, e_idx, N_EXPERTS), axis=0, keepdims=True)
        w2 = 1.0 / (1.0 + jnp.exp(v1 - v2))
        gates_ref[...] = jnp.where(e_idx == i1, 1.0 - w2, jnp.where(e_idx == i2, w2, 0.0))


def _merge(x2, ya, yb, yc, yd, g1, wg, bg, wb, wo, g2, router_t, tm):
    T, D = x2.shape
    with_router = router_t is not None
    full = lambda *shape: pl.BlockSpec(shape, lambda i: (0,) * len(shape))
    tokb = lambda w: pl.BlockSpec((tm, w), lambda i: (i, 0))
    in_specs = [tokb(D), tokb(BR), tokb(BR), tokb(BR), tokb(BR),
                full(1, D), full(D, N_BRANCH * D), full(1, N_BRANCH * D), full(N_BRANCH, BR, D),
                full(D, D), full(1, D)]
    args = [x2, ya, yb, yc, yd, g1, wg, bg, wb, wo, g2]
    out_specs = [tokb(D), tokb(D)]
    out_shape = [jax.ShapeDtypeStruct((T, D), F32), jax.ShapeDtypeStruct((T, D), BF16)]
    if with_router:
        in_specs.append(full(N_EXPERTS, D))
        args.append(router_t)
        out_specs.append(pl.BlockSpec((N_EXPERTS, tm), lambda i: (0, i)))
        out_shape.append(jax.ShapeDtypeStruct((N_EXPERTS, T), F32))
    return pl.pallas_call(
        functools.partial(_merge_kernel, with_router=with_router),
        grid=(T // tm,),
        in_specs=in_specs,
        out_specs=out_specs,
        out_shape=out_shape,
        compiler_params=pltpu.CompilerParams(
            dimension_semantics=("parallel",), vmem_limit_bytes=V7X_VMEM_LIMIT),
        name="merge_router" if with_router else "merge",
    )(*args)


def _ffn_kernel(x1_ref, h2_ref, w1_ref, w3_ref, w2_ref, o_ref):
    @pl.when(pl.program_id(1) == 0)
    def _():
        o_ref[...] = x1_ref[...]

    h2 = h2_ref[...]
    a = _dot(h2, w1_ref[...])
    mid = a * _sigmoid(a) * _dot(h2, w3_ref[...])
    o_ref[...] += _dot(mid.astype(BF16), w2_ref[...])


def _ffn(x1, h2, w1, w3, w2, tm, tf):
    T, D = x1.shape
    F = w1.shape[1]
    return pl.pallas_call(
        _ffn_kernel,
        grid=(T // tm, F // tf),
        in_specs=[pl.BlockSpec((tm, D), lambda i, f: (i, 0)),
                  pl.BlockSpec((tm, D), lambda i, f: (i, 0)),
                  pl.BlockSpec((D, tf), lambda i, f: (0, f)),
                  pl.BlockSpec((D, tf), lambda i, f: (0, f)),
                  pl.BlockSpec((tf, D), lambda i, f: (f, 0))],
        out_specs=pl.BlockSpec((tm, D), lambda i, f: (i, 0)),
        out_shape=jax.ShapeDtypeStruct((T, D), F32),
        compiler_params=pltpu.CompilerParams(
            dimension_semantics=("parallel", "arbitrary"), vmem_limit_bytes=V7X_VMEM_LIMIT),
        name="dense_ffn",
    )(x1, h2, w1, w3, w2)


ROW_ALIGN = 16
META_LANES = 128


def _moe_dims(T, tc, tm):
    n_ct = T // tc
    local_rows = 2 * tc + N_EXPERTS * ROW_ALIGN
    max_rows = 2 * T + n_ct * N_EXPERTS * (ROW_ALIGN - 1)
    n_mt = -(-max_rows // tm) + N_EXPERTS
    assert n_ct <= META_LANES and n_mt <= META_LANES and tm & (tm - 1) == 0
    return n_ct, local_rows, n_mt


def _sublane_excl_scan(x):
    rows, run = [], jnp.zeros_like(x[0:1, :])
    for e in range(x.shape[0]):
        rows.append(run)
        run = run + x[e:e + 1, :]
    return jnp.concatenate(rows, axis=0)


def _plan_kernel(gates_ref, member_ref, tri_ref, goff_ref, loc_ref, nch_ref, meta_ref, *, tm):
    routed = (gates_ref[...] > 0.0).astype(BF16)
    cnt = _dot(routed, member_ref[...]).astype(jnp.int32)
    row_sh, tm_sh = ROW_ALIGN.bit_length() - 1, tm.bit_length() - 1
    pad = ((cnt + (ROW_ALIGN - 1)) >> row_sh) << row_sh
    incl = jnp.dot(pad.astype(F32), tri_ref[...], preferred_element_type=F32,
                   precision=lax.Precision.HIGHEST).astype(jnp.int32)
    tot = jnp.max(incl, axis=1, keepdims=True)
    etot = ((tot + (tm - 1)) >> tm_sh) << tm_sh
    base = _sublane_excl_scan(etot)
    goff_ref[...] = base + incl - pad
    loc_ref[...] = _sublane_excl_scan(pad)
    nch_ref[...] = pad >> row_sh

    lane = lax.broadcasted_iota(jnp.int32, (N_EXPERTS, META_LANES), 1)
    sub = lax.broadcasted_iota(jnp.int32, (N_EXPERTS, META_LANES), 0)
    end = base + etot
    n_used = jnp.max(end, axis=0, keepdims=True) >> tm_sh
    src = jnp.minimum(lane, n_used - 1)
    expert = jnp.minimum(jnp.sum((src * tm >= end).astype(jnp.int32), axis=0, keepdims=True), N_EXPERTS - 1)
    start = lane * tm
    inside = (start >= base) & (start < end)
    valid = jnp.sum(jnp.where(inside, jnp.clip(tot - (start - base), 0, tm), 0), axis=0, keepdims=True)
    fill_start = jnp.sum(jnp.where(sub == lane, base + tot, 0), axis=0, keepdims=True)
    fill_nch = jnp.sum(jnp.where(sub == lane, (etot - tot) >> row_sh, 0), axis=0, keepdims=True)
    meta = jnp.where(sub == 0, expert, jnp.where(sub == 1, valid, jnp.where(sub == 2, src,
                     jnp.where(sub == 3, fill_start, jnp.where(sub == 4, fill_nch,
                     jnp.where(sub == 5, n_used, 0))))))
    meta_ref[...] = meta


def _moe_plan(gates_t, tc, tm):
    E, T = gates_t.shape
    member = (jnp.arange(T)[:, None] // tc == jnp.arange(META_LANES)[None, :]).astype(BF16)
    tri = (jnp.arange(META_LANES)[:, None] <= jnp.arange(META_LANES)[None, :]).astype(F32)
    tbl = jax.ShapeDtypeStruct((N_EXPERTS, META_LANES), jnp.int32)
    return pl.pallas_call(
        functools.partial(_plan_kernel, tm=tm),
        out_shape=[tbl, tbl, tbl, tbl],
        compiler_params=pltpu.CompilerParams(vmem_limit_bytes=V7X_VMEM_LIMIT),
        name="moe_plan",
    )(gates_t, member, tri)


def _segment_rows(gates, loc_s, tile, strict_upper):
    routed = gates > 0.0
    pos = _dot(routed.astype(BF16), strict_upper).astype(jnp.int32)
    sub = lax.broadcasted_iota(jnp.int32, gates.shape, 0)
    loc = jnp.zeros(gates.shape, jnp.int32)
    for e in range(N_EXPERTS):
        loc = jnp.where(sub == e, loc_s[e, tile], loc)
    rows = loc + pos
    row_a = jnp.min(jnp.where(routed, rows, jnp.iinfo(jnp.int32).max), axis=0, keepdims=True)
    row_b = jnp.max(jnp.where(routed, rows, -1), axis=0, keepdims=True)
    gate_a = jnp.sum(jnp.where(routed & (rows == row_a), gates, 0.0), axis=0, keepdims=True)
    gate_b = jnp.sum(jnp.where(routed & (rows == row_b), gates, 0.0), axis=0, keepdims=True)
    return row_a, row_b, gate_a, gate_b


def _chunk_copy(src, dst, src_row, dst_row, sem):
    aligned = lambda r: r if isinstance(r, int) else pl.multiple_of(r, ROW_ALIGN)
    return pltpu.make_async_copy(src.at[pl.ds(aligned(src_row), ROW_ALIGN)],
                                 dst.at[pl.ds(aligned(dst_row), ROW_ALIGN)], sem)


def _dispatch_kernel(goff_s, loc_s, nch_s, meta_s, gates_ref, h2_ref, upper_ref, xs_hbm,
                     xc_scr, zero_scr, sem, tile_sem):
    i = pl.program_id(0)
    n_ct = pl.num_programs(0)
    row_a, row_b, _, _ = _segment_rows(gates_ref[...], loc_s, i, upper_ref[...])
    r_iota = lax.broadcasted_iota(jnp.int32, (xc_scr.shape[1], gates_ref.shape[1]), 0)
    onehot = jnp.where(r_iota == row_a, 1.0, jnp.where(r_iota == row_b, 1.0, 0.0)).astype(BF16)
    slot = i % 2
    xc_scr[slot] = _dot(onehot, h2_ref[...]).astype(BF16)

    for e in range(N_EXPERTS):
        def issue(c, carry, e=e):
            _chunk_copy(xc_scr.at[slot], xs_hbm, loc_s[e, i] + c * ROW_ALIGN, goff_s[e, i] + c * ROW_ALIGN,
                        sem.at[slot]).start()
            return carry
        lax.fori_loop(0, nch_s[e, i], issue, 0)

    def drain_step(step, buf):
        def drain(c, carry):
            _chunk_copy(xc_scr.at[buf], xs_hbm, 0, 0, sem.at[buf]).wait()
            return carry
        lax.fori_loop(0, sum(nch_s[e, step] for e in range(N_EXPERTS)), drain, 0)

    @pl.when(i > 0)
    def _():
        drain_step(i - 1, 1 - slot)

    @pl.when(i == n_ct - 1)
    def _():
        drain_step(i, slot)
        zero_scr[...] = jnp.zeros(zero_scr.shape, zero_scr.dtype)
        fills = 0
        for e in range(N_EXPERTS):
            def fill(c, carry, e=e):
                _chunk_copy(zero_scr, xs_hbm, 0, meta_s[3, e] + c * ROW_ALIGN, sem.at[0]).start()
                return carry
            lax.fori_loop(0, meta_s[4, e], fill, 0)
            fills = fills + meta_s[4, e]

        def drain_fill(c, carry):
            _chunk_copy(zero_scr, xs_hbm, 0, 0, sem.at[0]).wait()
            return carry
        lax.fori_loop(0, fills, drain_fill, 0)

        tm = zero_scr.shape[0]
        n_mt = xs_hbm.shape[0] // tm

        def tile_copy(t):
            return pltpu.make_async_copy(zero_scr, xs_hbm.at[pl.ds(pl.multiple_of(t * tm, tm), tm)], tile_sem)

        def fill_tile(t, carry):
            tile_copy(t).start()
            return carry
        lax.fori_loop(meta_s[5, 0], n_mt, fill_tile, 0)

        def drain_tile(t, carry):
            tile_copy(t).wait()
            return carry
        lax.fori_loop(meta_s[5, 0], n_mt, drain_tile, 0)


def _moe_dispatch(tables, gates_t, h2, tc, tm):
    T, D = h2.shape
    n_ct, local_rows, n_mt = _moe_dims(T, tc, tm)
    upper = (jnp.arange(tc)[:, None] < jnp.arange(tc)[None, :]).astype(BF16)
    return pl.pallas_call(
        _dispatch_kernel,
        grid_spec=pltpu.PrefetchScalarGridSpec(
            num_scalar_prefetch=4,
            grid=(n_ct,),
            in_specs=[pl.BlockSpec((N_EXPERTS, tc), lambda i, *_: (0, i)),
                      pl.BlockSpec((tc, D), lambda i, *_: (i, 0)),
                      pl.BlockSpec((tc, tc), lambda i, *_: (0, 0))],
            out_specs=pl.BlockSpec(memory_space=pl.ANY),
            scratch_shapes=[pltpu.VMEM((2, local_rows, D), BF16), pltpu.VMEM((tm, D), BF16),
                            pltpu.SemaphoreType.DMA((2,)), pltpu.SemaphoreType.DMA(())]),
        out_shape=jax.ShapeDtypeStruct((n_mt * tm, D), BF16),
        compiler_params=pltpu.CompilerParams(
            dimension_semantics=("arbitrary",), vmem_limit_bytes=V7X_VMEM_LIMIT),
        name="moe_dispatch",
    )(*tables, gates_t, h2, upper)


def _experts_kernel(meta_s, xs_ref, w1_ref, w3_ref, w2_ref, ys_ref, acc_scr):
    m = pl.program_id(0)
    f = pl.program_id(1)

    @pl.when(meta_s[1, m] > 0)
    def _():
        x = xs_ref[...]
        a = _dot(x, w1_ref[0])
        mid = a * _sigmoid(a) * _dot(x, w3_ref[0])
        part = _dot(mid.astype(BF16), w2_ref[0])

        @pl.when(f == 0)
        def _():
            acc_scr[...] = part

        @pl.when(f > 0)
        def _():
            acc_scr[...] += part

        @pl.when(f == pl.num_programs(1) - 1)
        def _():
            ys_ref[...] = acc_scr[...].astype(BF16)

    @pl.when((meta_s[1, m] == 0) & (f == pl.num_programs(1) - 1))
    def _():
        ys_ref[...] = jnp.zeros(ys_ref.shape, ys_ref.dtype)


def _moe_experts(meta, xs, w1, w3, w2, tm, tf):
    rows, D = xs.shape
    F = w1.shape[2]
    nf = F // tf

    def f_eff(m, f, meta):
        return jnp.where(meta[1, m] > 0, f, nf - 1)

    return pl.pallas_call(
        _experts_kernel,
        grid_spec=pltpu.PrefetchScalarGridSpec(
            num_scalar_prefetch=1,
            grid=(rows // tm, nf),
            in_specs=[pl.BlockSpec((tm, D), lambda m, f, meta: (meta[2, m], 0)),
                      pl.BlockSpec((1, D, tf), lambda m, f, meta: (meta[0, m], 0, f_eff(m, f, meta))),
                      pl.BlockSpec((1, D, tf), lambda m, f, meta: (meta[0, m], 0, f_eff(m, f, meta))),
                      pl.BlockSpec((1, tf, D), lambda m, f, meta: (meta[0, m], f_eff(m, f, meta), 0))],
            out_specs=pl.BlockSpec((tm, D), lambda m, f, meta: (m, 0)),
            scratch_shapes=[pltpu.VMEM((tm, D), F32)]),
        out_shape=jax.ShapeDtypeStruct((rows, D), BF16),
        compiler_params=pltpu.CompilerParams(
            dimension_semantics=("arbitrary", "arbitrary"), vmem_limit_bytes=V7X_VMEM_LIMIT),
        name="moe_experts",
    )(meta, xs, w1, w3, w2)


def _combine_kernel(goff_s, loc_s, nch_s, gates_ref, x1_ref, upper_ref, ys_hbm, o_ref, yc_scr, sem):
    i = pl.program_id(0)
    tc = gates_ref.shape[1]

    slot = i % 2

    def fetch(step, buf):
        for e in range(N_EXPERTS):
            def issue(c, carry, e=e):
                _chunk_copy(ys_hbm, yc_scr.at[buf], goff_s[e, step] + c * ROW_ALIGN,
                            loc_s[e, step] + c * ROW_ALIGN, sem.at[buf]).start()
                return carry
            lax.fori_loop(0, nch_s[e, step], issue, 0)

    @pl.when(i == 0)
    def _():
        yc_scr[...] = jnp.zeros(yc_scr.shape, yc_scr.dtype)
        fetch(i, slot)

    @pl.when(i + 1 < pl.num_programs(0))
    def _():
        fetch(i + 1, 1 - slot)

    row_a, row_b, gate_a, gate_b = _segment_rows(gates_ref[...], loc_s, i, upper_ref[...])
    stats = jnp.concatenate([row_a.astype(F32), row_b.astype(F32), gate_a, gate_b,
                             jnp.zeros((META_LANES - 4, tc), F32)], axis=0).T
    lane = lax.broadcasted_iota(jnp.int32, (tc, yc_scr.shape[1]), 1).astype(F32)
    sel = jnp.where(lane == stats[:, 0:1], stats[:, 2:3],
                    jnp.where(lane == stats[:, 1:2], stats[:, 3:4], 0.0)).astype(BF16)

    def drain(c, carry):
        _chunk_copy(ys_hbm, yc_scr.at[slot], 0, 0, sem.at[slot]).wait()
        return carry
    lax.fori_loop(0, sum(nch_s[e, i] for e in range(N_EXPERTS)), drain, 0)
    o_ref[...] = x1_ref[...] + _dot(sel, yc_scr[slot])


def _moe_combine(tables, gates_t, x1, ys, tc, tm):
    T, D = x1.shape
    n_ct, local_rows, _ = _moe_dims(T, tc, tm)
    upper = (jnp.arange(tc)[:, None] < jnp.arange(tc)[None, :]).astype(BF16)
    return pl.pallas_call(
        _combine_kernel,
        grid_spec=pltpu.PrefetchScalarGridSpec(
            num_scalar_prefetch=3,
            grid=(n_ct,),
            in_specs=[pl.BlockSpec((N_EXPERTS, tc), lambda i, *_: (0, i)),
                      pl.BlockSpec((tc, D), lambda i, *_: (i, 0)),
                      pl.BlockSpec((tc, tc), lambda i, *_: (0, 0)),
                      pl.BlockSpec(memory_space=pl.ANY)],
            out_specs=pl.BlockSpec((tc, D), lambda i, *_: (i, 0)),
            scratch_shapes=[pltpu.VMEM((2, local_rows, D), BF16), pltpu.SemaphoreType.DMA((2,))]),
        out_shape=jax.ShapeDtypeStruct((T, D), F32),
        compiler_params=pltpu.CompilerParams(
            dimension_semantics=("arbitrary",), vmem_limit_bytes=V7X_VMEM_LIMIT),
        name="moe_combine",
    )(*tables[:3], gates_t, x1, upper, ys)


def _moe(x1, h2, gates_t, w1, w3, w2, tc, tm, tf):
    goff, loc, nch, meta = _moe_plan(gates_t, tc, tm)
    xs = _moe_dispatch((goff, loc, nch, meta), gates_t, h2, tc, tm)
    ys = _moe_experts(meta, xs, w1, w3, w2, tm, tf)
    return _moe_combine((goff, loc, nch), gates_t, x1, ys, tc, tm)


def _rope_tables(seq):
    pos = jnp.arange(seq, dtype=F32)
    inv = 1.0 / (ROPE_THETA ** (jnp.arange(0, QK_DIM, 2, dtype=F32) / QK_DIM))
    ang = pos[:, None] * inv[None, :]
    cos, sin = jnp.cos(ang), jnp.sin(ang)
    groups = BR // QK_DIM
    cos_t = jnp.tile(jnp.concatenate([cos, cos], axis=-1), (1, groups))
    sin_t = jnp.tile(jnp.concatenate([-sin, sin], axis=-1), (1, groups))
    return cos_t, sin_t


def _block_diag(blocks):
    g, n, m = blocks.shape
    eye = jnp.eye(g, dtype=blocks.dtype)
    return (eye[:, None, :, None] * blocks[:, :, None, :]).reshape(g * n, g * m)


def kernel(x, norm1_g, w_in, w_gate, b_gate, w_branch, w_out, pool_w, pool_scale, attn_qn_g, attn_kn_g,
           lam_q1, lam_k1, lam_q2, lam_k2, attn_subln_g, conv_w, conv_b, conv_ln_g, conv_ln_b,
           sgu_ln_g, sgu_ln_b, sgu_w, sgu_b, norm2_g, ffn_w1, ffn_w3, ffn_w2, moe_router,
           moe_w1, moe_w3, moe_w2):
    B, S, D = x.shape
    depth = w_in.shape[0]
    T = B * S
    tm_mix = min(512, S)
    tq = min(256, S)
    tm_merge = min(512, T)
    tm_ffn = min(1024, T)
    tm_moe = 512

    cos_t, sin_t = _rope_tables(S)
    bd32 = _block_diag(jnp.ones((BR // QK_DIM, QK_DIM, QK_DIM), BF16))
    row = lambda v: v.reshape(1, -1).astype(F32)

    for l in range(depth):
        x3 = x.reshape(B, S, D)
        q, k, vt, ya, yc, yd = _mixer_in(
            x3, row(norm1_g[l]), w_in[l].astype(BF16), cos_t, sin_t,
            row(jnp.tile(attn_qn_g[l], BR // QK_DIM)), row(jnp.tile(attn_kn_g[l], BR // QK_DIM)), bd32,
            _block_diag(pool_w[l]).astype(BF16), row(pool_scale[l]),
            conv_w[l], row(conv_b[l]), row(conv_ln_g[l]), row(conv_ln_b[l]),
            row(sgu_ln_g[l]), row(sgu_ln_b[l]), sgu_w[l],
            jnp.repeat(sgu_b[l].T, BR // SGU_GROUPS, axis=1), tm_mix)
        lamv = jnp.stack([lam_q1[l], lam_k1[l], lam_q2[l], lam_k2[l]]).astype(F32)
        lam_init = 0.8 - 0.6 * math.exp(-0.3 * l)
        sg_col = jnp.broadcast_to(attn_subln_g[l].astype(F32)[:, None], (V_DIM, tq))
        score_bound = (QK_DIM ** 0.5 * LOG2E) * jnp.max(jnp.abs(attn_qn_g[l])) * jnp.max(jnp.abs(attn_kn_g[l]))
        bounded = (score_bound <= SAFE_SCORE_BOUND).astype(jnp.int32).reshape(1)
        yb = _attention(bounded, lamv, q, k, vt, sg_col, lam_init, tq)

        is_moe = l % 2 == 1
        router_t = moe_router[l // 2].T.astype(F32) if is_moe else None
        flat = lambda y: y.reshape(T, BR)
        outs = _merge(x.reshape(T, D), flat(ya), flat(yb), flat(yc), flat(yd), row(norm1_g[l]),
                      w_gate[l].astype(BF16), row(b_gate[l]), w_branch[l].astype(BF16),
                      w_out[l].astype(BF16), row(norm2_g[l]), router_t, tm_merge)
        if is_moe:
            x1, h2, gates_t = outs
            i = l // 2
            x = _moe(x1, h2, gates_t, moe_w1[i].astype(BF16), moe_w3[i].astype(BF16), moe_w2[i].astype(BF16),
                     tm_merge, tm_moe, moe_w1.shape[3] // 2)
        else:
            x1, h2 = outs
            i = l // 2
            x = _ffn(x1, h2, ffn_w1[i].astype(BF16), ffn_w3[i].astype(BF16), ffn_w2[i].astype(BF16),
                     tm_ffn, ffn_w1.shape[2] // 2)
    return x.reshape(B, S, D)
```

```python
import functools
import math

import jax
import jax.numpy as jnp
from jax import lax
from jax.experimental import pallas as pl
from jax.experimental.pallas import tpu as pltpu

F32 = jnp.float32
BF16 = jnp.bfloat16

N_BRANCH = 4
BR = 256
POOL_WINDOWS = (2, 4, 8, 16)
POOL_GW = 64
HEADS = 4
QK_DIM = 32
V_DIM = 64
ROPE_THETA = 10000.0
CHUNK = 64
CONV_W = 31
SGU_LEN = 128
SGU_GROUPS = 4
N_EXPERTS = 8
EPS = 1e-6
QK_GROUPS = 2 * HEADS
LOG2E = math.log2(math.e)
SAFE_SCORE_BOUND = 40.0
NEG_INF = -1e30

HALO = 32
V7X_VMEM_LIMIT = 56 * 1024 * 1024


def _rms(x, g):
    return x * lax.rsqrt(jnp.mean(x * x, axis=-1, keepdims=True) + EPS) * g


def _ln(x, g, b):
    mu = jnp.mean(x, axis=-1, keepdims=True)
    xc = x - mu
    return xc * lax.rsqrt(jnp.mean(xc * xc, axis=-1, keepdims=True) + EPS) * g + b


def _sigmoid(x):
    return 1.0 / (1.0 + jnp.exp(-x))


def _dot(a, b):
    return jnp.dot(a, b, preferred_element_type=F32)


def _mixer_in_kernel(x_ref, xh_ref, g1_ref, win_ref, cos_ref, sin_ref, qg_ref, kg_ref, bd32_ref,
                     poolw_ref, pools_ref, convw_ref, convb_ref, clng_ref, clnb_ref,
                     slng_ref, slnb_ref, sguw_ref, sgub_ref,
                     q_ref, k_ref, vt_ref, ya_ref, yc_ref, yd_ref,
                     a_scr, s2_scr, s4_scr, s8_scr, u_scr, us_scr):
    i = pl.program_id(1)
    tm = x_ref.shape[1]
    g1 = g1_ref[...]
    h = _rms(x_ref[0], g1).astype(BF16)
    proj = _dot(h, win_ref[0])

    hh = _rms(xh_ref[0], g1).astype(BF16)
    halo_ok = (i > 0).astype(F32)
    pa_h = _dot(hh, win_ref[0, :, 0:BR]) * halo_ok
    pc_h = _dot(hh, win_ref[0, :, 4 * BR:6 * BR]) * halo_ok

    a = proj[:, 0:BR]
    a_scr[0:HALO, :] = pa_h
    a_scr[HALO:HALO + tm, :] = a
    n_ext = tm + HALO
    s2_scr[8:n_ext, :] = a_scr[8:n_ext, :] + a_scr[7:n_ext - 1, :]
    s4_scr[16:n_ext, :] = s2_scr[16:n_ext, :] + s2_scr[14:n_ext - 2, :]
    s8_scr[24:n_ext, :] = s4_scr[24:n_ext, :] + s4_scr[20:n_ext - 4, :]
    s16 = s8_scr[HALO:n_ext, :] + s8_scr[HALO - 8:n_ext - 8, :]
    lane = lax.broadcasted_iota(jnp.int32, (tm, BR), 1)
    row = lax.broadcasted_iota(jnp.int32, (tm, BR), 0)
    win_sum = jnp.where(lane < POOL_GW, s2_scr[HALO:n_ext, :],
                        jnp.where(lane < 2 * POOL_GW, s4_scr[HALO:n_ext, :],
                                  jnp.where(lane < 3 * POOL_GW, s8_scr[HALO:n_ext, :], s16)))
    width = jnp.where(lane < POOL_GW, POOL_WINDOWS[0],
                      jnp.where(lane < 2 * POOL_GW, POOL_WINDOWS[1],
                                jnp.where(lane < 3 * POOL_GW, POOL_WINDOWS[2], POOL_WINDOWS[3])))
    cnt = jnp.minimum(width, i * tm + row + 1).astype(F32)
    pooled = win_sum / cnt - a
    ya_ref[0] = (_dot(pooled.astype(BF16), poolw_ref[...]) * pools_ref[...]).astype(BF16)

    lo_half = (lane % QK_DIM) < (QK_DIM // 2)

    def qk_norm_rope(t, g_ref, scale):
        ss = _dot((t * t).astype(BF16), bd32_ref[...])
        tn = t * lax.rsqrt(ss * (1.0 / QK_DIM) + EPS) * g_ref[...]
        rot = jnp.where(lo_half, pltpu.roll(tn, BR - QK_DIM // 2, 1), pltpu.roll(tn, QK_DIM // 2, 1))
        return (tn * cos_ref[...] + rot * sin_ref[...]) * scale

    qq = qk_norm_rope(proj[:, BR:2 * BR], qg_ref, QK_DIM ** -0.5 * LOG2E).astype(BF16)
    kk = qk_norm_rope(proj[:, 2 * BR:3 * BR], kg_ref, 1.0).astype(BF16)
    for g in range(QK_GROUPS):
        q_ref[0, g] = qq[:, g * QK_DIM:(g + 1) * QK_DIM]
        k_ref[0, g] = kk[:, g * QK_DIM:(g + 1) * QK_DIM]
    vt_ref[0] = proj[:, 3 * BR:4 * BR].T.astype(BF16)

    u_scr[0:HALO, :] = pc_h[:, 0:BR] * _sigmoid(pc_h[:, BR:2 * BR])
    u_scr[HALO:HALO + tm, :] = proj[:, 4 * BR:5 * BR] * _sigmoid(proj[:, 5 * BR:6 * BR])
    taps = [convw_ref[j:j + 1, :] for j in range(CONV_W)]
    convb = convb_ref[...]
    clng = clng_ref[...]
    clnb = clnb_ref[...]
    n_sh = us_scr.shape[1]
    for s in range(1, 8):
        us_scr[s - 1] = u_scr[s:s + n_sh, :]
    rc = 64
    base = HALO - (CONV_W - 1)

    def tap_window(c, j):
        s, r0 = (base + j) % 8, c * rc + (base + j) // 8 * 8
        return u_scr[r0:r0 + rc, :] if s == 0 else us_scr[s - 1, r0:r0 + rc, :]

    for c in range(tm // rc):
        acc = convb + taps[0] * tap_window(c, 0)
        for j in range(1, CONV_W):
            acc = acc + taps[j] * tap_window(c, j)
        y = _ln(acc, clng, clnb)
        yc_ref[0, c * rc:(c + 1) * rc, :] = (y * _sigmoid(y)).astype(BF16)

    d_in = proj[:, 6 * BR:8 * BR]
    z = 0.5 * d_in * (1.0 + lax.erf(d_in * (2.0 ** -0.5)))
    vv = _ln(z[:, BR:2 * BR], slng_ref[...], slnb_ref[...]).astype(BF16)
    tr = lax.broadcasted_iota(jnp.int32, (SGU_LEN, SGU_LEN), 0)
    tc = lax.broadcasted_iota(jnp.int32, (SGU_LEN, SGU_LEN), 1)
    wtri = [jnp.where(tr >= tc, sguw_ref[g], 0.0).astype(BF16) for g in range(SGU_GROUPS)]
    lane_g = lax.broadcasted_iota(jnp.int32, (SGU_LEN, BR), 1) // (BR // SGU_GROUPS)
    sgub = sgub_ref[...]
    for n in range(tm // SGU_LEN):
        vn = vv[n * SGU_LEN:(n + 1) * SGU_LEN, :]
        sv = _dot(wtri[0], vn)
        for g in range(1, SGU_GROUPS):
            sv = jnp.where(lane_g == g, _dot(wtri[g], vn), sv)
        yd_ref[0, n * SGU_LEN:(n + 1) * SGU_LEN, :] = (
            z[n * SGU_LEN:(n + 1) * SGU_LEN, 0:BR] * (sv + sgub)).astype(BF16)


def _mixer_in(x3, g1, win, win_l, cos_t, sin_t, qg, kg, bd32, poolw, pools, convw, convb, clng, clnb,
              slng, slnb, sguw, sgub, tm):
    B, S, D = x3.shape
    nt = S // tm
    hb = tm // HALO
    full = lambda *shape: pl.BlockSpec(shape, lambda b, i: (0,) * len(shape))
    in_specs = [
        pl.BlockSpec((1, tm, D), lambda b, i: (b, i, 0)),
        pl.BlockSpec((1, HALO, D), lambda b, i: (b, jnp.maximum(i * hb - 1, 0), 0)),
        full(1, D), pl.BlockSpec((1, D, 8 * BR), lambda b, i: (win_l, 0, 0)),
        pl.BlockSpec((tm, BR), lambda b, i: (i, 0)),
        pl.BlockSpec((tm, BR), lambda b, i: (i, 0)),
        full(1, BR), full(1, BR), full(BR, BR),
        full(BR, BR), full(1, BR), full(CONV_W, BR), full(1, BR), full(1, BR), full(1, BR),
        full(1, BR), full(1, BR), full(SGU_GROUPS, SGU_LEN, SGU_LEN), full(SGU_LEN, BR),
    ]
    tok = lambda: pl.BlockSpec((1, tm, BR), lambda b, i: (b, i, 0))
    split = lambda: pl.BlockSpec((1, QK_GROUPS, tm, QK_DIM), lambda b, i: (b, 0, i, 0))
    out_specs = [
        split(), split(),
        pl.BlockSpec((1, BR, tm), lambda b, i: (b, 0, i)),
        tok(), tok(), tok(),
    ]
    out_shape = [
        jax.ShapeDtypeStruct((B, QK_GROUPS, S, QK_DIM), BF16),
        jax.ShapeDtypeStruct((B, QK_GROUPS, S, QK_DIM), BF16),
        jax.ShapeDtypeStruct((B, BR, S), BF16),
        jax.ShapeDtypeStruct((B, S, BR), BF16),
        jax.ShapeDtypeStruct((B, S, BR), BF16),
        jax.ShapeDtypeStruct((B, S, BR), BF16),
    ]
    ext = pltpu.VMEM((tm + HALO, BR), F32)
    return pl.pallas_call(
        _mixer_in_kernel,
        grid=(B, nt),
        in_specs=in_specs,
        out_specs=out_specs,
        out_shape=out_shape,
        scratch_shapes=[ext, ext, ext, ext, ext, pltpu.VMEM((7, tm + HALO - 8, BR), F32)],
        compiler_params=pltpu.CompilerParams(
            dimension_semantics=("parallel", "parallel"), vmem_limit_bytes=V7X_VMEM_LIMIT),
        name="mixer_in",
    )(x3, x3, g1, win, cos_t, sin_t, qg, kg, bd32, poolw, pools, convw, convb, clng, clnb,
      slng, slnb, sguw, sgub)


def _attn_kernel(bounded_s, lam_ref, q_ref, k_ref, vt_ref, sg_ref, *rest, lam_init, tq, n_side):
    o_ref, p_scr = rest[n_side], rest[-1]
    for w_ref, wb_ref in zip(rest[:n_side], rest[n_side + 1:2 * n_side + 1]):
        wb_ref[...] = w_ref[...].astype(BF16)
    i = pl.program_id(1)
    key_chunk = lax.broadcasted_iota(jnp.int32, (tq, tq), 0) // CHUNK
    qry_chunk = lax.broadcasted_iota(jnp.int32, (tq, tq), 1) // CHUNK
    diag_mask = key_chunk <= qry_chunk

    def scores(g, j, masked):
        k0 = pl.multiple_of(j * tq, tq)
        st = lax.dot_general(k_ref[0, g, pl.ds(k0, tq), :], q_ref[0, g], (((1,), (1,)), ((), ())),
                             preferred_element_type=F32)
        return jnp.where(diag_mask, st, NEG_INF) if masked else st

    def fold8(x, op):
        return op(x.reshape(tq // 8, 8, tq), axis=0)

    def exact_maxima():
        def max_tile(j, ms, masked):
            return tuple(jnp.maximum(ms[g], fold8(scores(g, j, masked), jnp.max)) for g in range(QK_GROUPS))

        ms = tuple(jnp.full((8, tq), NEG_INF, F32) for _ in range(QK_GROUPS))
        ms = lax.fori_loop(0, i, functools.partial(max_tile, masked=False), ms)
        ms = max_tile(i, ms, True)
        return [jnp.max(x, axis=0, keepdims=True) for x in ms]

    def softmax_pv(shift):
        def acc_tile(j, carry, masked):
            k0 = pl.multiple_of(j * tq, tq)
            ls = []
            for g in range(QK_GROUPS):
                st = scores(g, j, masked)
                p = jnp.exp2(st if shift is None else st - shift[g])
                p_scr[g] = p.astype(BF16)
                ls.append(carry[g][0] + fold8(p, jnp.sum))
            out = []
            for g in range(QK_GROUPS):
                hd = g // 2
                vt = vt_ref[0, hd * V_DIM:(hd + 1) * V_DIM, pl.ds(k0, tq)]
                out.append((ls[g], carry[g][1] + _dot(vt, p_scr[g])))
            return tuple(out)

        init = tuple((jnp.zeros((8, tq), F32), jnp.zeros((V_DIM, tq), F32)) for _ in range(QK_GROUPS))
        carry = lax.fori_loop(0, i, functools.partial(acc_tile, masked=False), init)
        carry = acc_tile(i, carry, True)
        return [(jnp.sum(l8, axis=0, keepdims=True), acc) for l8, acc in carry]

    def finish(res):
        lamv = lam_ref[...]
        lam = (jnp.exp(jnp.sum(lamv[0:1] * lamv[1:2], axis=-1, keepdims=True))
               - jnp.exp(jnp.sum(lamv[2:3] * lamv[3:4], axis=-1, keepdims=True)) + lam_init)
        sg = sg_ref[...]
        heads = []
        for hd in range(HEADS):
            o = res[2 * hd][1] / res[2 * hd][0] - lam * (res[2 * hd + 1][1] / res[2 * hd + 1][0])
            o = o * lax.rsqrt(jnp.mean(o * o, axis=0, keepdims=True) + EPS) * sg * (1.0 - lam_init)
            heads.append(o)
        o_ref[0] = jnp.concatenate(heads, axis=0).T.astype(BF16)

    @pl.when(bounded_s[0] == 1)
    def _():
        finish(softmax_pv(None))

    @pl.when(bounded_s[0] != 1)
    def _():
        finish(softmax_pv(exact_maxima()))


def _cast_rows_per_block(rows, steps):
    for rpb in range(ROW_ALIGN, rows + 1, ROW_ALIGN):
        if rows % rpb == 0 and rows // rpb <= steps:
            return rpb
    raise ValueError(f"no aligned row block for {rows} rows in {steps} steps")


def _attention(bounded, lamv, q, k, vt, sg_col, lam_init, tq, side=()):
    B, _, S, _ = q.shape
    nq = S // tq
    side2d = [w.reshape(-1, w.shape[-1]) for w in side]
    side_specs = []
    for w in side2d:
        rpb = _cast_rows_per_block(w.shape[0], B * nq)
        last = w.shape[0] // rpb - 1
        side_specs.append(pl.BlockSpec(
            (rpb, w.shape[1]), lambda b, i, *_, last=last: (jnp.minimum(b * nq + i, last), 0)))
    outs = pl.pallas_call(
        functools.partial(_attn_kernel, lam_init=lam_init, tq=tq, n_side=len(side2d)),
        grid_spec=pltpu.PrefetchScalarGridSpec(
            num_scalar_prefetch=1,
            grid=(B, nq),
            in_specs=[
                pl.BlockSpec((4, QK_DIM), lambda b, i, *_: (0, 0)),
                pl.BlockSpec((1, QK_GROUPS, tq, QK_DIM), lambda b, i, *_: (b, 0, i, 0)),
                pl.BlockSpec((1, QK_GROUPS, S, QK_DIM), lambda b, i, *_: (b, 0, 0, 0)),
                pl.BlockSpec((1, BR, S), lambda b, i, *_: (b, 0, 0)),
                pl.BlockSpec((V_DIM, tq), lambda b, i, *_: (0, 0)),
            ] + side_specs,
            out_specs=[pl.BlockSpec((1, tq, BR), lambda b, i, *_: (b, i, 0))] + side_specs,
            scratch_shapes=[pltpu.VMEM((QK_GROUPS, tq, tq), BF16)]),
        out_shape=[jax.ShapeDtypeStruct((B, S, BR), BF16)]
                  + [jax.ShapeDtypeStruct(w.shape, BF16) for w in side2d],
        compiler_params=pltpu.CompilerParams(
            dimension_semantics=("arbitrary", "arbitrary"), vmem_limit_bytes=V7X_VMEM_LIMIT),
        name="diff_attn",
    )(bounded, lamv, q, k, vt, sg_col, *side2d)
    return outs[0], [o.reshape(w.shape) for o, w in zip(outs[1:], side)]


def _merge_kernel(*refs, with_router):
    if with_router:
        (x_ref, ya_ref, yb_ref, yc_ref, yd_ref, g1_ref, wg_ref, bg_ref, wb_ref, wo_ref, g2_ref, rt_ref,
         x1_ref, h2_ref, gates_ref) = refs
    else:
        (x_ref, ya_ref, yb_ref, yc_ref, yd_ref, g1_ref, wg_ref, bg_ref, wb_ref, wo_ref, g2_ref,
         x1_ref, h2_ref) = refs
    x = x_ref[...]
    d = x.shape[1]
    h = _rms(x, g1_ref[...]).astype(BF16)
    merged = None
    for b, y_ref in enumerate((ya_ref, yb_ref, yc_ref, yd_ref)):
        gate = _sigmoid(_dot(h, wg_ref[0, :, b * d:(b + 1) * d]) + bg_ref[:, b * d:(b + 1) * d])
        term = gate * _dot(y_ref[...], wb_ref[0, b])
        merged = term if merged is None else merged + term
    x1 = x + _dot(merged.astype(BF16), wo_ref[0])
    x1_ref[...] = x1
    h2 = _rms(x1, g2_ref[...])
    h2_ref[...] = h2.astype(BF16)
    if with_router:
        def split(t):
            hi = t.astype(BF16)
            return hi, (t - hi.astype(F32)).astype(BF16)

        def nt_dot(a, b):
            return lax.dot_general(a, b, (((1,), (1,)), ((), ())), preferred_element_type=F32)

        (r_hi, r_lo), (h_hi, h_lo) = split(rt_ref[...]), split(h2)
        logits = nt_dot(r_hi, h_hi) + (nt_dot(r_hi, h_lo) + nt_dot(r_lo, h_hi))
        e_idx = lax.broadcasted_iota(jnp.int32, logits.shape, 0)
        v1 = jnp.max(logits, axis=0, keepdims=True)
        i1 = jnp.min(jnp.where(logits == v1, e_idx, N_EXPERTS), axis=0, keepdims=True)
        rest = jnp.where(e_idx == i1, -jnp.inf, logits)
        v2 = jnp.max(rest, axis=0, keepdims=True)
        i2 = jnp.min(jnp.where(rest == v2, e_idx, N_EXPERTS), axis=0, keepdims=True)
        w2 = 1.0 / (1.0 + jnp.exp(v1 - v2))
        gates_ref[...] = jnp.where(e_idx == i1, 1.0 - w2, jnp.where(e_idx == i2, w2, 0.0))


def _merge(x2, ya, yb, yc, yd, g1, wg, bg, wb, wo, layer, g2, router_t, tm):
    T, D = x2.shape
    with_router = router_t is not None
    full = lambda *shape: pl.BlockSpec(shape, lambda i: (0,) * len(shape))
    of_layer = lambda *shape: pl.BlockSpec((1,) + shape, lambda i: (layer,) + (0,) * len(shape))
    tokb = lambda w: pl.BlockSpec((tm, w), lambda i: (i, 0))
    in_specs = [tokb(D), tokb(BR), tokb(BR), tokb(BR), tokb(BR),
                full(1, D), of_layer(D, N_BRANCH * D), full(1, N_BRANCH * D), of_layer(N_BRANCH, BR, D),
                of_layer(D, D), full(1, D)]
    args = [x2, ya, yb, yc, yd, g1, wg, bg, wb, wo, g2]
    out_specs = [tokb(D), tokb(D)]
    out_shape = [jax.ShapeDtypeStruct((T, D), F32), jax.ShapeDtypeStruct((T, D), BF16)]
    if with_router:
        in_specs.append(full(N_EXPERTS, D))
        args.append(router_t)
        out_specs.append(pl.BlockSpec((N_EXPERTS, tm), lambda i: (0, i)))
        out_shape.append(jax.ShapeDtypeStruct((N_EXPERTS, T), F32))
    return pl.pallas_call(
        functools.partial(_merge_kernel, with_router=with_router),
        grid=(T // tm,),
        in_specs=in_specs,
        out_specs=out_specs,
        out_shape=out_shape,
        compiler_params=pltpu.CompilerParams(
            dimension_semantics=("parallel",), vmem_limit_bytes=V7X_VMEM_LIMIT),
        name="merge_router" if with_router else "merge",
    )(*args)


def _ffn_kernel(x1_ref, h2_ref, w1_ref, w3_ref, w2_ref, o_ref):
    @pl.when(pl.program_id(1) == 0)
    def _():
        o_ref[...] = x1_ref[...]

    h2 = h2_ref[...]
    a = _dot(h2, w1_ref[0])
    mid = a * _sigmoid(a) * _dot(h2, w3_ref[0])
    o_ref[...] += _dot(mid.astype(BF16), w2_ref[0])


def _ffn(x1, h2, w1, w3, w2, which, tm, tf):
    T, D = x1.shape
    F = w1.shape[2]
    return pl.pallas_call(
        _ffn_kernel,
        grid=(T // tm, F // tf),
        in_specs=[pl.BlockSpec((tm, D), lambda i, f: (i, 0)),
                  pl.BlockSpec((tm, D), lambda i, f: (i, 0)),
                  pl.BlockSpec((1, D, tf), lambda i, f: (which, 0, f)),
                  pl.BlockSpec((1, D, tf), lambda i, f: (which, 0, f)),
                  pl.BlockSpec((1, tf, D), lambda i, f: (which, f, 0))],
        out_specs=pl.BlockSpec((tm, D), lambda i, f: (i, 0)),
        out_shape=jax.ShapeDtypeStruct((T, D), F32),
        compiler_params=pltpu.CompilerParams(
            dimension_semantics=("parallel", "arbitrary"), vmem_limit_bytes=V7X_VMEM_LIMIT),
        name="dense_ffn",
    )(x1, h2, w1, w3, w2)


ROW_ALIGN = 16
META_LANES = 128


def _moe_dims(T, tc, tm):
    n_ct = T // tc
    local_rows = 2 * tc + N_EXPERTS * ROW_ALIGN
    max_rows = 2 * T + n_ct * N_EXPERTS * (ROW_ALIGN - 1)
    n_mt = -(-max_rows // tm) + N_EXPERTS
    assert n_ct <= META_LANES and n_mt <= META_LANES and tm & (tm - 1) == 0
    return n_ct, local_rows, n_mt


def _sublane_excl_scan(x):
    rows, run = [], jnp.zeros_like(x[0:1, :])
    for e in range(x.shape[0]):
        rows.append(run)
        run = run + x[e:e + 1, :]
    return jnp.concatenate(rows, axis=0)


def _plan_kernel(gates_ref, member_ref, tri_ref, goff_ref, loc_ref, nch_ref, meta_ref, *, tm):
    routed = (gates_ref[...] > 0.0).astype(BF16)
    cnt = _dot(routed, member_ref[...]).astype(jnp.int32)
    row_sh, tm_sh = ROW_ALIGN.bit_length() - 1, tm.bit_length() - 1
    pad = ((cnt + (ROW_ALIGN - 1)) >> row_sh) << row_sh
    incl = jnp.dot(pad.astype(F32), tri_ref[...], preferred_element_type=F32,
                   precision=lax.Precision.HIGHEST).astype(jnp.int32)
    tot = jnp.max(incl, axis=1, keepdims=True)
    etot = ((tot + (tm - 1)) >> tm_sh) << tm_sh
    base = _sublane_excl_scan(etot)
    goff_ref[...] = base + incl - pad
    loc_ref[...] = _sublane_excl_scan(pad)
    nch_ref[...] = pad >> row_sh

    lane = lax.broadcasted_iota(jnp.int32, (N_EXPERTS, META_LANES), 1)
    sub = lax.broadcasted_iota(jnp.int32, (N_EXPERTS, META_LANES), 0)
    end = base + etot
    n_used = jnp.max(end, axis=0, keepdims=True) >> tm_sh
    src = jnp.minimum(lane, n_used - 1)
    expert = jnp.minimum(jnp.sum((src * tm >= end).astype(jnp.int32), axis=0, keepdims=True), N_EXPERTS - 1)
    start = lane * tm
    inside = (start >= base) & (start < end)
    valid = jnp.sum(jnp.where(inside, jnp.clip(tot - (start - base), 0, tm), 0), axis=0, keepdims=True)
    fill_start = jnp.sum(jnp.where(sub == lane, base + tot, 0), axis=0, keepdims=True)
    fill_nch = jnp.sum(jnp.where(sub == lane, (etot - tot) >> row_sh, 0), axis=0, keepdims=True)
    meta = jnp.where(sub == 0, expert, jnp.where(sub == 1, valid, jnp.where(sub == 2, src,
                     jnp.where(sub == 3, fill_start, jnp.where(sub == 4, fill_nch,
                     jnp.where(sub == 5, n_used, 0))))))
    meta_ref[...] = meta


def _moe_plan(gates_t, tc, tm):
    E, T = gates_t.shape
    member = (jnp.arange(T)[:, None] // tc == jnp.arange(META_LANES)[None, :]).astype(BF16)
    tri = (jnp.arange(META_LANES)[:, None] <= jnp.arange(META_LANES)[None, :]).astype(F32)
    tbl = jax.ShapeDtypeStruct((N_EXPERTS, META_LANES), jnp.int32)
    return pl.pallas_call(
        functools.partial(_plan_kernel, tm=tm),
        out_shape=[tbl, tbl, tbl, tbl],
        compiler_params=pltpu.CompilerParams(vmem_limit_bytes=V7X_VMEM_LIMIT),
        name="moe_plan",
    )(gates_t, member, tri)


def _segment_rows(gates, loc_s, tile, strict_upper):
    routed = gates > 0.0
    pos = _dot(routed.astype(BF16), strict_upper).astype(jnp.int32)
    sub = lax.broadcasted_iota(jnp.int32, gates.shape, 0)
    loc = jnp.zeros(gates.shape, jnp.int32)
    for e in range(N_EXPERTS):
        loc = jnp.where(sub == e, loc_s[e, tile], loc)
    rows = loc + pos
    row_a = jnp.min(jnp.where(routed, rows, jnp.iinfo(jnp.int32).max), axis=0, keepdims=True)
    row_b = jnp.max(jnp.where(routed, rows, -1), axis=0, keepdims=True)
    gate_a = jnp.sum(jnp.where(routed & (rows == row_a), gates, 0.0), axis=0, keepdims=True)
    gate_b = jnp.sum(jnp.where(routed & (rows == row_b), gates, 0.0), axis=0, keepdims=True)
    return row_a, row_b, gate_a, gate_b


def _chunk_copy(src, dst, src_row, dst_row, sem):
    aligned = lambda r: r if isinstance(r, int) else pl.multiple_of(r, ROW_ALIGN)
    return pltpu.make_async_copy(src.at[pl.ds(aligned(src_row), ROW_ALIGN)],
                                 dst.at[pl.ds(aligned(dst_row), ROW_ALIGN)], sem)


def _dispatch_kernel(goff_s, loc_s, nch_s, meta_s, gates_ref, h2_ref, upper_ref, xs_hbm,
                     xc_scr, zero_scr, sem, tile_sem):
    i = pl.program_id(0)
    n_ct = pl.num_programs(0)
    row_a, row_b, _, _ = _segment_rows(gates_ref[...], loc_s, i, upper_ref[...])
    r_iota = lax.broadcasted_iota(jnp.int32, (xc_scr.shape[1], gates_ref.shape[1]), 0)
    onehot = jnp.where(r_iota == row_a, 1.0, jnp.where(r_iota == row_b, 1.0, 0.0)).astype(BF16)
    slot = i % 2
    xc_scr[slot] = _dot(onehot, h2_ref[...]).astype(BF16)

    for e in range(N_EXPERTS):
        def issue(c, carry, e=e):
            _chunk_copy(xc_scr.at[slot], xs_hbm, loc_s[e, i] + c * ROW_ALIGN, goff_s[e, i] + c * ROW_ALIGN,
                        sem.at[slot]).start()
            return carry
        lax.fori_loop(0, nch_s[e, i], issue, 0)

    def drain_step(step, buf):
        def drain(c, carry):
            _chunk_copy(xc_scr.at[buf], xs_hbm, 0, 0, sem.at[buf]).wait()
            return carry
        lax.fori_loop(0, sum(nch_s[e, step] for e in range(N_EXPERTS)), drain, 0)

    @pl.when(i > 0)
    def _():
        drain_step(i - 1, 1 - slot)

    @pl.when(i == n_ct - 1)
    def _():
        drain_step(i, slot)
        zero_scr[...] = jnp.zeros(zero_scr.shape, zero_scr.dtype)
        fills = 0
        for e in range(N_EXPERTS):
            def fill(c, carry, e=e):
                _chunk_copy(zero_scr, xs_hbm, 0, meta_s[3, e] + c * ROW_ALIGN, sem.at[0]).start()
                return carry
            lax.fori_loop(0, meta_s[4, e], fill, 0)
            fills = fills + meta_s[4, e]

        def drain_fill(c, carry):
            _chunk_copy(zero_scr, xs_hbm, 0, 0, sem.at[0]).wait()
            return carry
        lax.fori_loop(0, fills, drain_fill, 0)

        tm = zero_scr.shape[0]
        n_mt = xs_hbm.shape[0] // tm

        def tile_copy(t):
            return pltpu.make_async_copy(zero_scr, xs_hbm.at[pl.ds(pl.multiple_of(t * tm, tm), tm)], tile_sem)

        def fill_tile(t, carry):
            tile_copy(t).start()
            return carry
        lax.fori_loop(meta_s[5, 0], n_mt, fill_tile, 0)

        def drain_tile(t, carry):
            tile_copy(t).wait()
            return carry
        lax.fori_loop(meta_s[5, 0], n_mt, drain_tile, 0)


def _moe_dispatch(tables, gates_t, h2, tc, tm):
    T, D = h2.shape
    n_ct, local_rows, n_mt = _moe_dims(T, tc, tm)
    upper = (jnp.arange(tc)[:, None] < jnp.arange(tc)[None, :]).astype(BF16)
    return pl.pallas_call(
        _dispatch_kernel,
        grid_spec=pltpu.PrefetchScalarGridSpec(
            num_scalar_prefetch=4,
            grid=(n_ct,),
            in_specs=[pl.BlockSpec((N_EXPERTS, tc), lambda i, *_: (0, i)),
                      pl.BlockSpec((tc, D), lambda i, *_: (i, 0)),
                      pl.BlockSpec((tc, tc), lambda i, *_: (0, 0))],
            out_specs=pl.BlockSpec(memory_space=pl.ANY),
            scratch_shapes=[pltpu.VMEM((2, local_rows, D), BF16), pltpu.VMEM((tm, D), BF16),
                            pltpu.SemaphoreType.DMA((2,)), pltpu.SemaphoreType.DMA(())]),
        out_shape=jax.ShapeDtypeStruct((n_mt * tm, D), BF16),
        compiler_params=pltpu.CompilerParams(
            dimension_semantics=("arbitrary",), vmem_limit_bytes=V7X_VMEM_LIMIT),
        name="moe_dispatch",
    )(*tables, gates_t, h2, upper)


def _experts_kernel(meta_s, xs_ref, w1_ref, w3_ref, w2_ref, ys_ref, acc_scr):
    m = pl.program_id(0)
    f = pl.program_id(1)

    @pl.when(meta_s[1, m] > 0)
    def _():
        x = xs_ref[...]
        a = _dot(x, w1_ref[0])
        mid = a * _sigmoid(a) * _dot(x, w3_ref[0])
        part = _dot(mid.astype(BF16), w2_ref[0])

        @pl.when(f == 0)
        def _():
            acc_scr[...] = part

        @pl.when(f > 0)
        def _():
            acc_scr[...] += part

        @pl.when(f == pl.num_programs(1) - 1)
        def _():
            ys_ref[...] = acc_scr[...].astype(BF16)

    @pl.when((meta_s[1, m] == 0) & (f == pl.num_programs(1) - 1))
    def _():
        ys_ref[...] = jnp.zeros(ys_ref.shape, ys_ref.dtype)


def _moe_experts(meta, xs, w1, w3, w2, which, tm, tf):
    rows, D = xs.shape
    F = w1.shape[3]
    nf = F // tf
    w1, w3, w2 = (w.reshape((-1,) + w.shape[2:]) for w in (w1, w3, w2))
    meta = meta.at[0].add(which * N_EXPERTS)

    def f_eff(m, f, meta):
        return jnp.where(meta[1, m] > 0, f, nf - 1)

    return pl.pallas_call(
        _experts_kernel,
        grid_spec=pltpu.PrefetchScalarGridSpec(
            num_scalar_prefetch=1,
            grid=(rows // tm, nf),
            in_specs=[pl.BlockSpec((tm, D), lambda m, f, meta: (meta[2, m], 0)),
                      pl.BlockSpec((1, D, tf), lambda m, f, meta: (meta[0, m], 0, f_eff(m, f, meta))),
                      pl.BlockSpec((1, D, tf), lambda m, f, meta: (meta[0, m], 0, f_eff(m, f, meta))),
                      pl.BlockSpec((1, tf, D), lambda m, f, meta: (meta[0, m], f_eff(m, f, meta), 0))],
            out_specs=pl.BlockSpec((tm, D), lambda m, f, meta: (m, 0)),
            scratch_shapes=[pltpu.VMEM((tm, D), F32)]),
        out_shape=jax.ShapeDtypeStruct((rows, D), BF16),
        compiler_params=pltpu.CompilerParams(
            dimension_semantics=("arbitrary", "arbitrary"), vmem_limit_bytes=V7X_VMEM_LIMIT),
        name="moe_experts",
    )(meta, xs, w1, w3, w2)


def _combine_kernel(goff_s, loc_s, nch_s, gates_ref, x1_ref, upper_ref, ys_hbm, o_ref, yc_scr, sem):
    i = pl.program_id(0)
    tc = gates_ref.shape[1]

    slot = i % 2

    def fetch(step, buf):
        for e in range(N_EXPERTS):
            def issue(c, carry, e=e):
                _chunk_copy(ys_hbm, yc_scr.at[buf], goff_s[e, step] + c * ROW_ALIGN,
                            loc_s[e, step] + c * ROW_ALIGN, sem.at[buf]).start()
                return carry
            lax.fori_loop(0, nch_s[e, step], issue, 0)

    @pl.when(i == 0)
    def _():
        yc_scr[...] = jnp.zeros(yc_scr.shape, yc_scr.dtype)
        fetch(i, slot)

    @pl.when(i + 1 < pl.num_programs(0))
    def _():
        fetch(i + 1, 1 - slot)

    row_a, row_b, gate_a, gate_b = _segment_rows(gates_ref[...], loc_s, i, upper_ref[...])
    stats = jnp.concatenate([row_a.astype(F32), row_b.astype(F32), gate_a, gate_b,
                             jnp.zeros((META_LANES - 4, tc), F32)], axis=0).T
    lane = lax.broadcasted_iota(jnp.int32, (tc, yc_scr.shape[1]), 1).astype(F32)
    sel = jnp.where(lane == stats[:, 0:1], stats[:, 2:3],
                    jnp.where(lane == stats[:, 1:2], stats[:, 3:4], 0.0)).astype(BF16)

    def drain(c, carry):
        _chunk_copy(ys_hbm, yc_scr.at[slot], 0, 0, sem.at[slot]).wait()
        return carry
    lax.fori_loop(0, sum(nch_s[e, i] for e in range(N_EXPERTS)), drain, 0)
    o_ref[...] = x1_ref[...] + _dot(sel, yc_scr[slot])


def _moe_combine(tables, gates_t, x1, ys, tc, tm):
    T, D = x1.shape
    n_ct, local_rows, _ = _moe_dims(T, tc, tm)
    upper = (jnp.arange(tc)[:, None] < jnp.arange(tc)[None, :]).astype(BF16)
    return pl.pallas_call(
        _combine_kernel,
        grid_spec=pltpu.PrefetchScalarGridSpec(
            num_scalar_prefetch=3,
            grid=(n_ct,),
            in_specs=[pl.BlockSpec((N_EXPERTS, tc), lambda i, *_: (0, i)),
                      pl.BlockSpec((tc, D), lambda i, *_: (i, 0)),
                      pl.BlockSpec((tc, tc), lambda i, *_: (0, 0)),
                      pl.BlockSpec(memory_space=pl.ANY)],
            out_specs=pl.BlockSpec((tc, D), lambda i, *_: (i, 0)),
            scratch_shapes=[pltpu.VMEM((2, local_rows, D), BF16), pltpu.SemaphoreType.DMA((2,))]),
        out_shape=jax.ShapeDtypeStruct((T, D), F32),
        compiler_params=pltpu.CompilerParams(
            dimension_semantics=("arbitrary",), vmem_limit_bytes=V7X_VMEM_LIMIT),
        name="moe_combine",
    )(*tables[:3], gates_t, x1, upper, ys)


def _moe(x1, h2, gates_t, w1, w3, w2, which, tc, tm, tf):
    goff, loc, nch, meta = _moe_plan(gates_t, tc, tm)
    xs = _moe_dispatch((goff, loc, nch, meta), gates_t, h2, tc, tm)
    ys = _moe_experts(meta, xs, w1, w3, w2, which, tm, tf)
    return _moe_combine((goff, loc, nch), gates_t, x1, ys, tc, tm)


def _rope_tables(seq):
    pos = jnp.arange(seq, dtype=F32)
    inv = 1.0 / (ROPE_THETA ** (jnp.arange(0, QK_DIM, 2, dtype=F32) / QK_DIM))
    ang = pos[:, None] * inv[None, :]
    cos, sin = jnp.cos(ang), jnp.sin(ang)
    groups = BR // QK_DIM
    cos_t = jnp.tile(jnp.concatenate([cos, cos], axis=-1), (1, groups))
    sin_t = jnp.tile(jnp.concatenate([-sin, sin], axis=-1), (1, groups))
    return cos_t, sin_t


def _block_diag(blocks):
    g, n, m = blocks.shape
    eye = jnp.eye(g, dtype=blocks.dtype)
    return (eye[:, None, :, None] * blocks[:, :, None, :]).reshape(g * n, g * m)


def kernel(x, norm1_g, w_in, w_gate, b_gate, w_branch, w_out, pool_w, pool_scale, attn_qn_g, attn_kn_g,
           lam_q1, lam_k1, lam_q2, lam_k2, attn_subln_g, conv_w, conv_b, conv_ln_g, conv_ln_b,
           sgu_ln_g, sgu_ln_b, sgu_w, sgu_b, norm2_g, ffn_w1, ffn_w3, ffn_w2, moe_router,
           moe_w1, moe_w3, moe_w2):
    B, S, D = x.shape
    depth = w_in.shape[0]
    T = B * S
    tm_mix = min(512, S)
    tq = min(256, S)
    tm_merge = min(512, T)
    tm_ffn = min(1024, T)
    tm_moe = 512

    cos_t, sin_t = _rope_tables(S)
    bd32 = _block_diag(jnp.ones((BR // QK_DIM, QK_DIM, QK_DIM), BF16))
    row = lambda v: v.reshape(1, -1).astype(F32)

    stacked = {"w_in": w_in, "w_gate": w_gate, "w_branch": w_branch, "w_out": w_out, "ffn_w1": ffn_w1,
               "ffn_w3": ffn_w3, "ffn_w2": ffn_w2, "moe_w1": moe_w1, "moe_w3": moe_w3, "moe_w2": moe_w2}
    first_in = w_in[0:1].astype(BF16)
    bf = {}

    for l in range(depth):
        x3 = x.reshape(B, S, D)
        q, k, vt, ya, yc, yd = _mixer_in(
            x3, row(norm1_g[l]), first_in if l == 0 else bf["w_in"], 0 if l == 0 else l, cos_t, sin_t,
            row(jnp.tile(attn_qn_g[l], BR // QK_DIM)), row(jnp.tile(attn_kn_g[l], BR // QK_DIM)), bd32,
            _block_diag(pool_w[l]).astype(BF16), row(pool_scale[l]),
            conv_w[l], row(conv_b[l]), row(conv_ln_g[l]), row(conv_ln_b[l]),
            row(sgu_ln_g[l]), row(sgu_ln_b[l]), sgu_w[l],
            jnp.repeat(sgu_b[l].T, BR // SGU_GROUPS, axis=1), tm_mix)
        lamv = jnp.stack([lam_q1[l], lam_k1[l], lam_q2[l], lam_k2[l]]).astype(F32)
        lam_init = 0.8 - 0.6 * math.exp(-0.3 * l)
        sg_col = jnp.broadcast_to(attn_subln_g[l].astype(F32)[:, None], (V_DIM, tq))
        score_bound = (QK_DIM ** 0.5 * LOG2E) * jnp.max(jnp.abs(attn_qn_g[l])) * jnp.max(jnp.abs(attn_kn_g[l]))
        bounded = (score_bound <= SAFE_SCORE_BOUND).astype(jnp.int32).reshape(1)
        yb, cast = _attention(bounded, lamv, q, k, vt, sg_col, lam_init, tq,
                              side=list(stacked.values()) if l == 0 else ())
        if l == 0:
            bf = dict(zip(stacked, cast))

        is_moe = l % 2 == 1
        router_t = moe_router[l // 2].T.astype(F32) if is_moe else None
        flat = lambda y: y.reshape(T, BR)
        outs = _merge(x.reshape(T, D), flat(ya), flat(yb), flat(yc), flat(yd), row(norm1_g[l]),
                      bf["w_gate"], row(b_gate[l]), bf["w_branch"], bf["w_out"], l, row(norm2_g[l]),
                      router_t, tm_merge)
        i = l // 2
        if is_moe:
            x1, h2, gates_t = outs
            x = _moe(x1, h2, gates_t, bf["moe_w1"], bf["moe_w3"], bf["moe_w2"], i,
                     tm_merge, tm_moe, moe_w1.shape[3] // 2)
        else:
            x1, h2 = outs
            x = _ffn(x1, h2, bf["ffn_w1"], bf["ffn_w3"], bf["ffn_w2"], i, tm_ffn, ffn_w1.shape[2] // 2)
    return x.reshape(B, S, D)
```

```python
import functools
import math

import jax
import jax.numpy as jnp
from jax import lax
from jax.experimental import pallas as pl
from jax.experimental.pallas import tpu as pltpu

F32 = jnp.float32
BF16 = jnp.bfloat16

N_BRANCH = 4
BR = 256
POOL_WINDOWS = (2, 4, 8, 16)
POOL_GW = 64
HEADS = 4
QK_DIM = 32
V_DIM = 64
ROPE_THETA = 10000.0
CHUNK = 64
CONV_W = 31
SGU_LEN = 128
SGU_GROUPS = 4
N_EXPERTS = 8
EPS = 1e-6
QK_GROUPS = 2 * HEADS
LOG2E = math.log2(math.e)
FFN_CHUNK = 1024
KV_TILES_PER_STEP = 4
SAFE_SCORE_BOUND = 40.0
NEG_INF = -1e30

HALO = 32
V7X_VMEM_LIMIT = 56 * 1024 * 1024


def _rms(x, g):
    return x * lax.rsqrt(jnp.mean(x * x, axis=-1, keepdims=True) + EPS) * g


def _ln(x, g, b):
    mu = jnp.mean(x, axis=-1, keepdims=True)
    xc = x - mu
    return xc * lax.rsqrt(jnp.mean(xc * xc, axis=-1, keepdims=True) + EPS) * g + b


def _sigmoid(x):
    return 1.0 / (1.0 + jnp.exp(-x))


def _dot(a, b):
    return jnp.dot(a, b, preferred_element_type=F32)


def _mixer_in_kernel(x_ref, xh_ref, g1_ref, win_ref, cos_ref, sin_ref, qg_ref, kg_ref, bd32_ref,
                     poolw_ref, pools_ref, convw_ref, convb_ref, clng_ref, clnb_ref,
                     slng_ref, slnb_ref, sguw_ref, sgub_ref,
                     q_ref, k_ref, vt_ref, ya_ref, yc_ref, yd_ref,
                     a_scr, s2_scr, s4_scr, s8_scr, u_scr, us_scr):
    i = pl.program_id(1)
    tm = x_ref.shape[1]
    g1 = g1_ref[...]
    h = _rms(x_ref[0], g1).astype(BF16)
    proj = _dot(h, win_ref[0])

    hh = _rms(xh_ref[0], g1).astype(BF16)
    halo_ok = (i > 0).astype(F32)
    pa_h = _dot(hh, win_ref[0, :, 0:BR]) * halo_ok
    pc_h = _dot(hh, win_ref[0, :, 4 * BR:6 * BR]) * halo_ok

    a = proj[:, 0:BR]
    a_scr[0:HALO, :] = pa_h
    a_scr[HALO:HALO + tm, :] = a
    n_ext = tm + HALO
    s2_scr[8:n_ext, :] = a_scr[8:n_ext, :] + a_scr[7:n_ext - 1, :]
    s4_scr[16:n_ext, :] = s2_scr[16:n_ext, :] + s2_scr[14:n_ext - 2, :]
    s8_scr[24:n_ext, :] = s4_scr[24:n_ext, :] + s4_scr[20:n_ext - 4, :]
    s16 = s8_scr[HALO:n_ext, :] + s8_scr[HALO - 8:n_ext - 8, :]
    lane = lax.broadcasted_iota(jnp.int32, (tm, BR), 1)
    row = lax.broadcasted_iota(jnp.int32, (tm, BR), 0)
    win_sum = jnp.where(lane < POOL_GW, s2_scr[HALO:n_ext, :],
                        jnp.where(lane < 2 * POOL_GW, s4_scr[HALO:n_ext, :],
                                  jnp.where(lane < 3 * POOL_GW, s8_scr[HALO:n_ext, :], s16)))
    width = jnp.where(lane < POOL_GW, POOL_WINDOWS[0],
                      jnp.where(lane < 2 * POOL_GW, POOL_WINDOWS[1],
                                jnp.where(lane < 3 * POOL_GW, POOL_WINDOWS[2], POOL_WINDOWS[3])))
    cnt = jnp.minimum(width, i * tm + row + 1).astype(F32)
    pooled = win_sum / cnt - a
    ya_ref[0] = (_dot(pooled.astype(BF16), poolw_ref[...]) * pools_ref[...]).astype(BF16)

    lo_half = (lane % QK_DIM) < (QK_DIM // 2)

    def qk_norm_rope(t, g_ref, scale):
        ss = _dot((t * t).astype(BF16), bd32_ref[...])
        tn = t * lax.rsqrt(ss * (1.0 / QK_DIM) + EPS) * g_ref[...]
        rot = jnp.where(lo_half, pltpu.roll(tn, BR - QK_DIM // 2, 1), pltpu.roll(tn, QK_DIM // 2, 1))
        return (tn * cos_ref[...] + rot * sin_ref[...]) * scale

    q_ref[0] = qk_norm_rope(proj[:, BR:2 * BR], qg_ref, QK_DIM ** -0.5 * LOG2E).T.astype(BF16)
    kk = qk_norm_rope(proj[:, 2 * BR:3 * BR], kg_ref, 1.0).astype(BF16)
    for g in range(QK_GROUPS):
        k_ref[0, g] = kk[:, g * QK_DIM:(g + 1) * QK_DIM]
    vt_ref[0] = proj[:, 3 * BR:4 * BR].T.astype(BF16)

    u_scr[0:HALO, :] = pc_h[:, 0:BR] * _sigmoid(pc_h[:, BR:2 * BR])
    u_scr[HALO:HALO + tm, :] = proj[:, 4 * BR:5 * BR] * _sigmoid(proj[:, 5 * BR:6 * BR])
    taps = [convw_ref[j:j + 1, :] for j in range(CONV_W)]
    convb = convb_ref[...]
    clng = clng_ref[...]
    clnb = clnb_ref[...]
    n_sh = us_scr.shape[1]
    for s in range(1, 8):
        us_scr[s - 1] = u_scr[s:s + n_sh, :]
    rc = 64
    base = HALO - (CONV_W - 1)

    def tap_window(c, j):
        s, r0 = (base + j) % 8, c * rc + (base + j) // 8 * 8
        return u_scr[r0:r0 + rc, :] if s == 0 else us_scr[s - 1, r0:r0 + rc, :]

    for c in range(tm // rc):
        acc = convb + taps[0] * tap_window(c, 0)
        for j in range(1, CONV_W):
            acc = acc + taps[j] * tap_window(c, j)
        y = _ln(acc, clng, clnb)
        yc_ref[0, c * rc:(c + 1) * rc, :] = (y * _sigmoid(y)).astype(BF16)

    d_in = proj[:, 6 * BR:8 * BR]
    z = 0.5 * d_in * (1.0 + lax.erf(d_in * (2.0 ** -0.5)))
    vv = _ln(z[:, BR:2 * BR], slng_ref[...], slnb_ref[...]).astype(BF16)
    tr = lax.broadcasted_iota(jnp.int32, (SGU_LEN, SGU_LEN), 0)
    tc = lax.broadcasted_iota(jnp.int32, (SGU_LEN, SGU_LEN), 1)
    wtri = [jnp.where(tr >= tc, sguw_ref[g], 0.0).astype(BF16) for g in range(SGU_GROUPS)]
    lane_g = lax.broadcasted_iota(jnp.int32, (SGU_LEN, BR), 1) // (BR // SGU_GROUPS)
    sgub = sgub_ref[...]
    for n in range(tm // SGU_LEN):
        vn = vv[n * SGU_LEN:(n + 1) * SGU_LEN, :]
        sv = _dot(wtri[0], vn)
        for g in range(1, SGU_GROUPS):
            sv = jnp.where(lane_g == g, _dot(wtri[g], vn), sv)
        yd_ref[0, n * SGU_LEN:(n + 1) * SGU_LEN, :] = (
            z[n * SGU_LEN:(n + 1) * SGU_LEN, 0:BR] * (sv + sgub)).astype(BF16)


def _mixer_in(x3, g1, win, win_l, cos_t, sin_t, qg, kg, bd32, poolw, pools, convw, convb, clng, clnb,
              slng, slnb, sguw, sgub, tm):
    B, S, D = x3.shape
    nt = S // tm
    hb = tm // HALO
    full = lambda *shape: pl.BlockSpec(shape, lambda b, i: (0,) * len(shape))
    in_specs = [
        pl.BlockSpec((1, tm, D), lambda b, i: (b, i, 0)),
        pl.BlockSpec((1, HALO, D), lambda b, i: (b, jnp.maximum(i * hb - 1, 0), 0)),
        full(1, D), pl.BlockSpec((1, D, 8 * BR), lambda b, i: (win_l, 0, 0)),
        pl.BlockSpec((tm, BR), lambda b, i: (i, 0)),
        pl.BlockSpec((tm, BR), lambda b, i: (i, 0)),
        full(1, BR), full(1, BR), full(BR, BR),
        full(BR, BR), full(1, BR), full(CONV_W, BR), full(1, BR), full(1, BR), full(1, BR),
        full(1, BR), full(1, BR), full(SGU_GROUPS, SGU_LEN, SGU_LEN), full(SGU_LEN, BR),
    ]
    tok = lambda: pl.BlockSpec((1, tm, BR), lambda b, i: (b, i, 0))
    split = lambda: pl.BlockSpec((1, QK_GROUPS, tm, QK_DIM), lambda b, i: (b, 0, i, 0))
    flipped = lambda: pl.BlockSpec((1, BR, tm), lambda b, i: (b, 0, i))
    out_specs = [
        flipped(), split(), flipped(),
        tok(), tok(), tok(),
    ]
    out_shape = [
        jax.ShapeDtypeStruct((B, BR, S), BF16),
        jax.ShapeDtypeStruct((B, QK_GROUPS, S, QK_DIM), BF16),
        jax.ShapeDtypeStruct((B, BR, S), BF16),
        jax.ShapeDtypeStruct((B, S, BR), BF16),
        jax.ShapeDtypeStruct((B, S, BR), BF16),
        jax.ShapeDtypeStruct((B, S, BR), BF16),
    ]
    ext = pltpu.VMEM((tm + HALO, BR), F32)
    return pl.pallas_call(
        _mixer_in_kernel,
        grid=(B, nt),
        in_specs=in_specs,
        out_specs=out_specs,
        out_shape=out_shape,
        scratch_shapes=[ext, ext, ext, ext, ext, pltpu.VMEM((7, tm + HALO - 8, BR), F32)],
        compiler_params=pltpu.CompilerParams(
            dimension_semantics=("parallel", "parallel"), vmem_limit_bytes=V7X_VMEM_LIMIT),
        name="mixer_in",
    )(x3, x3, g1, win, cos_t, sin_t, qg, kg, bd32, poolw, pools, convw, convb, clng, clnb,
      slng, slnb, sguw, sgub)


def _attn_kernel(bounded_s, lam_ref, q_ref, k_ref, vt_ref, sg_ref, *rest, lam_init, tq, n_side):
    o_ref, p_scr = rest[n_side], rest[-1]
    for w_ref, wb_ref in zip(rest[:n_side], rest[n_side + 1:2 * n_side + 1]):
        wb_ref[...] = w_ref[...].astype(BF16)
    i = pl.program_id(1)
    key_chunk = lax.broadcasted_iota(jnp.int32, (tq, tq), 0) // CHUNK
    qry_chunk = lax.broadcasted_iota(jnp.int32, (tq, tq), 1) // CHUNK
    diag_mask = key_chunk <= qry_chunk

    def scores(g, j, masked):
        k0 = pl.multiple_of(j * tq, tq)
        st = _dot(k_ref[0, g, pl.ds(k0, tq), :], q_ref[0, g * QK_DIM:(g + 1) * QK_DIM, :])
        return jnp.where(diag_mask, st, NEG_INF) if masked else st

    def fold8(x, op):
        return op(x.reshape(tq // 8, 8, tq), axis=0)

    def exact_maxima():
        def max_tile(j, ms, masked):
            return tuple(jnp.maximum(ms[g], fold8(scores(g, j, masked), jnp.max)) for g in range(QK_GROUPS))

        ms = tuple(jnp.full((8, tq), NEG_INF, F32) for _ in range(QK_GROUPS))
        ms = lax.fori_loop(0, i, functools.partial(max_tile, masked=False), ms)
        ms = max_tile(i, ms, True)
        return [jnp.max(x, axis=0, keepdims=True) for x in ms]

    def softmax_pv(shift):
        def acc_tile(j, carry, masked, buf=0):
            k0 = pl.multiple_of(j * tq, tq)
            ls = []
            for g in range(QK_GROUPS):
                st = scores(g, j, masked)
                p = jnp.exp2(st if shift is None else st - shift[g])
                p_scr[buf, g] = p.astype(BF16)
                ls.append(carry[g][0] + fold8(p, jnp.sum))
            out = []
            for g in range(QK_GROUPS):
                hd = g // 2
                vt = vt_ref[0, hd * V_DIM:(hd + 1) * V_DIM, pl.ds(k0, tq)]
                out.append((ls[g], carry[g][1] + _dot(vt, p_scr[buf, g])))
            return tuple(out)

        n_buf = p_scr.shape[0]

        def acc_group(jj, carry):
            for t in range(n_buf):
                carry = acc_tile(n_buf * jj + t, carry, False, t)
            return carry

        init = tuple((jnp.zeros((8, tq), F32), jnp.zeros((V_DIM, tq), F32)) for _ in range(QK_GROUPS))
        carry = lax.fori_loop(0, i // n_buf, acc_group, init)
        carry = lax.fori_loop(i - i % n_buf, i, functools.partial(acc_tile, masked=False), carry)
        carry = acc_tile(i, carry, True)
        return [(jnp.sum(l8, axis=0, keepdims=True), acc) for l8, acc in carry]

    def finish(res):
        lamv = lam_ref[...]
        lam = (jnp.exp(jnp.sum(lamv[0:1] * lamv[1:2], axis=-1, keepdims=True))
               - jnp.exp(jnp.sum(lamv[2:3] * lamv[3:4], axis=-1, keepdims=True)) + lam_init)
        sg = sg_ref[...]
        heads = []
        for hd in range(HEADS):
            o = res[2 * hd][1] / res[2 * hd][0] - lam * (res[2 * hd + 1][1] / res[2 * hd + 1][0])
            o = o * lax.rsqrt(jnp.mean(o * o, axis=0, keepdims=True) + EPS) * sg * (1.0 - lam_init)
            heads.append(o)
        o_ref[0] = jnp.concatenate(heads, axis=0).T.astype(BF16)

    @pl.when(bounded_s[0] == 1)
    def _():
        finish(softmax_pv(None))

    @pl.when(bounded_s[0] != 1)
    def _():
        finish(softmax_pv(exact_maxima()))


def _cast_rows_per_block(rows, steps):
    for rpb in range(ROW_ALIGN, rows + 1, ROW_ALIGN):
        if rows % rpb == 0 and rows // rpb <= steps:
            return rpb
    raise ValueError(f"no aligned row block for {rows} rows in {steps} steps")


def _attention(bounded, lamv, q, k, vt, sg_col, lam_init, tq, side=()):
    B, _, S = q.shape
    nq = S // tq
    side2d = [w.reshape(-1, w.shape[-1]) for w in side]
    side_specs = []
    for w in side2d:
        rpb = _cast_rows_per_block(w.shape[0], B * nq)
        last = w.shape[0] // rpb - 1
        side_specs.append(pl.BlockSpec(
            (rpb, w.shape[1]), lambda b, i, *_, last=last: (jnp.minimum(b * nq + i, last), 0)))
    outs = pl.pallas_call(
        functools.partial(_attn_kernel, lam_init=lam_init, tq=tq, n_side=len(side2d)),
        grid_spec=pltpu.PrefetchScalarGridSpec(
            num_scalar_prefetch=1,
            grid=(B, nq),
            in_specs=[
                pl.BlockSpec((4, QK_DIM), lambda b, i, *_: (0, 0)),
                pl.BlockSpec((1, BR, tq), lambda b, i, *_: (b, 0, i)),
                pl.BlockSpec((1, QK_GROUPS, S, QK_DIM), lambda b, i, *_: (b, 0, 0, 0)),
                pl.BlockSpec((1, BR, S), lambda b, i, *_: (b, 0, 0)),
                pl.BlockSpec((V_DIM, tq), lambda b, i, *_: (0, 0)),
            ] + side_specs,
            out_specs=[pl.BlockSpec((1, tq, BR), lambda b, i, *_: (b, i, 0))] + side_specs,
            scratch_shapes=[pltpu.VMEM((KV_TILES_PER_STEP, QK_GROUPS, tq, tq), BF16)]),
        out_shape=[jax.ShapeDtypeStruct((B, S, BR), BF16)]
                  + [jax.ShapeDtypeStruct(w.shape, BF16) for w in side2d],
        compiler_params=pltpu.CompilerParams(
            dimension_semantics=("arbitrary", "arbitrary"), vmem_limit_bytes=V7X_VMEM_LIMIT),
        name="diff_attn",
    )(bounded, lamv, q, k, vt, sg_col, *side2d)
    return outs[0], [o.reshape(w.shape) for o, w in zip(outs[1:], side)]


def _merge_kernel(*refs, with_router):
    if with_router:
        (x_ref, ya_ref, yb_ref, yc_ref, yd_ref, g1_ref, wg_ref, bg_ref, wb_ref, wo_ref, g2_ref, rt_ref,
         x1_ref, h2_ref, gates_ref) = refs
    else:
        (x_ref, ya_ref, yb_ref, yc_ref, yd_ref, g1_ref, wg_ref, bg_ref, wb_ref, wo_ref, g2_ref,
         x1_ref, h2_ref) = refs
    x = x_ref[...]
    d = x.shape[1]
    h = _rms(x, g1_ref[...]).astype(BF16)
    merged = None
    for b, y_ref in enumerate((ya_ref, yb_ref, yc_ref, yd_ref)):
        gate = _sigmoid(_dot(h, wg_ref[0, :, b * d:(b + 1) * d]) + bg_ref[:, b * d:(b + 1) * d])
        term = gate * _dot(y_ref[...], wb_ref[0, b])
        merged = term if merged is None else merged + term
    x1 = x + _dot(merged.astype(BF16), wo_ref[0])
    x1_ref[...] = x1
    h2 = _rms(x1, g2_ref[...])
    h2_ref[...] = h2.astype(BF16)
    if with_router:
        def split(t):
            hi = t.astype(BF16)
            return hi, (t - hi.astype(F32)).astype(BF16)

        def nt_dot(a, b):
            return lax.dot_general(a, b, (((1,), (1,)), ((), ())), preferred_element_type=F32)

        (r_hi, r_lo), (h_hi, h_lo) = split(rt_ref[...]), split(h2)
        logits = nt_dot(r_hi, h_hi) + (nt_dot(r_hi, h_lo) + nt_dot(r_lo, h_hi))
        e_idx = lax.broadcasted_iota(jnp.int32, logits.shape, 0)
        v1 = jnp.max(logits, axis=0, keepdims=True)
        i1 = jnp.min(jnp.where(logits == v1, e_idx, N_EXPERTS), axis=0, keepdims=True)
        rest = jnp.where(e_idx == i1, -jnp.inf, logits)
        v2 = jnp.max(rest, axis=0, keepdims=True)
        i2 = jnp.min(jnp.where(rest == v2, e_idx, N_EXPERTS), axis=0, keepdims=True)
        w2 = 1.0 / (1.0 + jnp.exp(v1 - v2))
        gates_ref[...] = jnp.where(e_idx == i1, 1.0 - w2, jnp.where(e_idx == i2, w2, 0.0))


def _merge(x2, ya, yb, yc, yd, g1, wg, bg, wb, wo, layer, g2, router_t, tm):
    T, D = x2.shape
    with_router = router_t is not None
    full = lambda *shape: pl.BlockSpec(shape, lambda i: (0,) * len(shape))
    of_layer = lambda *shape: pl.BlockSpec((1,) + shape, lambda i: (layer,) + (0,) * len(shape))
    tokb = lambda w: pl.BlockSpec((tm, w), lambda i: (i, 0))
    in_specs = [tokb(D), tokb(BR), tokb(BR), tokb(BR), tokb(BR),
                full(1, D), of_layer(D, N_BRANCH * D), full(1, N_BRANCH * D), of_layer(N_BRANCH, BR, D),
                of_layer(D, D), full(1, D)]
    args = [x2, ya, yb, yc, yd, g1, wg, bg, wb, wo, g2]
    out_specs = [tokb(D), tokb(D)]
    out_shape = [jax.ShapeDtypeStruct((T, D), F32), jax.ShapeDtypeStruct((T, D), BF16)]
    if with_router:
        in_specs.append(full(N_EXPERTS, D))
        args.append(router_t)
        out_specs.append(pl.BlockSpec((N_EXPERTS, tm), lambda i: (0, i)))
        out_shape.append(jax.ShapeDtypeStruct((N_EXPERTS, T), F32))
    return pl.pallas_call(
        functools.partial(_merge_kernel, with_router=with_router),
        grid=(T // tm,),
        in_specs=in_specs,
        out_specs=out_specs,
        out_shape=out_shape,
        compiler_params=pltpu.CompilerParams(
            dimension_semantics=("parallel",), vmem_limit_bytes=V7X_VMEM_LIMIT),
        name="merge_router" if with_router else "merge",
    )(*args)


def _ffn_kernel(x1_ref, h2_ref, w1_ref, w3_ref, w2_ref, o_ref, *, tf):
    h2 = h2_ref[...]
    acc = x1_ref[...]
    for c0 in range(0, w1_ref.shape[2], tf):
        c1 = min(c0 + tf, w1_ref.shape[2])
        a = _dot(h2, w1_ref[0, :, c0:c1])
        mid = a * _sigmoid(a) * _dot(h2, w3_ref[0, :, c0:c1])
        acc = acc + _dot(mid.astype(BF16), w2_ref[0, c0:c1, :])
    o_ref[...] = acc


def _ffn(x1, h2, w1, w3, w2, which, tm, tf):
    T, D = x1.shape
    F = w1.shape[2]
    resident = lambda *shape: pl.BlockSpec((1,) + shape, lambda i: (which, 0, 0), pipeline_mode=pl.Buffered(1))
    return pl.pallas_call(
        functools.partial(_ffn_kernel, tf=tf),
        grid=(T // tm,),
        in_specs=[pl.BlockSpec((tm, D), lambda i: (i, 0)),
                  pl.BlockSpec((tm, D), lambda i: (i, 0)),
                  resident(D, F), resident(D, F), resident(F, D)],
        out_specs=pl.BlockSpec((tm, D), lambda i: (i, 0)),
        out_shape=jax.ShapeDtypeStruct((T, D), F32),
        compiler_params=pltpu.CompilerParams(
            dimension_semantics=("parallel",), vmem_limit_bytes=V7X_VMEM_LIMIT),
        name="dense_ffn",
    )(x1, h2, w1, w3, w2)


ROW_ALIGN = 16
META_LANES = 128


def _moe_dims(T, tc, tm):
    n_ct = T // tc
    local_rows = 2 * tc + N_EXPERTS * ROW_ALIGN
    max_rows = 2 * T + n_ct * N_EXPERTS * (ROW_ALIGN - 1)
    n_mt = -(-max_rows // tm) + N_EXPERTS
    assert n_ct <= META_LANES and n_mt <= META_LANES and tm & (tm - 1) == 0
    return n_ct, local_rows, n_mt


def _sublane_excl_scan(x):
    rows, run = [], jnp.zeros_like(x[0:1, :])
    for e in range(x.shape[0]):
        rows.append(run)
        run = run + x[e:e + 1, :]
    return jnp.concatenate(rows, axis=0)


def _plan_kernel(gates_ref, member_ref, tri_ref, goff_ref, loc_ref, nch_ref, meta_ref, *, tm):
    routed = (gates_ref[...] > 0.0).astype(BF16)
    cnt = _dot(routed, member_ref[...]).astype(jnp.int32)
    row_sh, tm_sh = ROW_ALIGN.bit_length() - 1, tm.bit_length() - 1
    pad = ((cnt + (ROW_ALIGN - 1)) >> row_sh) << row_sh
    incl = jnp.dot(pad.astype(F32), tri_ref[...], preferred_element_type=F32,
                   precision=lax.Precision.HIGHEST).astype(jnp.int32)
    tot = jnp.max(incl, axis=1, keepdims=True)
    etot = ((tot + (tm - 1)) >> tm_sh) << tm_sh
    base = _sublane_excl_scan(etot)
    goff_ref[...] = base + incl - pad
    loc_ref[...] = _sublane_excl_scan(pad)
    nch_ref[...] = pad >> row_sh

    lane = lax.broadcasted_iota(jnp.int32, (N_EXPERTS, META_LANES), 1)
    sub = lax.broadcasted_iota(jnp.int32, (N_EXPERTS, META_LANES), 0)
    end = base + etot
    n_used = jnp.max(end, axis=0, keepdims=True) >> tm_sh
    src = jnp.minimum(lane, n_used - 1)
    expert = jnp.minimum(jnp.sum((src * tm >= end).astype(jnp.int32), axis=0, keepdims=True), N_EXPERTS - 1)
    start = lane * tm
    inside = (start >= base) & (start < end)
    valid = jnp.sum(jnp.where(inside, jnp.clip(tot - (start - base), 0, tm), 0), axis=0, keepdims=True)
    fill_start = jnp.sum(jnp.where(sub == lane, base + tot, 0), axis=0, keepdims=True)
    fill_nch = jnp.sum(jnp.where(sub == lane, (etot - tot) >> row_sh, 0), axis=0, keepdims=True)
    meta = jnp.where(sub == 0, expert, jnp.where(sub == 1, valid, jnp.where(sub == 2, src,
                     jnp.where(sub == 3, fill_start, jnp.where(sub == 4, fill_nch,
                     jnp.where(sub == 5, n_used, 0))))))
    meta_ref[...] = meta


def _moe_plan(gates_t, tc, tm):
    E, T = gates_t.shape
    member = (jnp.arange(T)[:, None] // tc == jnp.arange(META_LANES)[None, :]).astype(BF16)
    tri = (jnp.arange(META_LANES)[:, None] <= jnp.arange(META_LANES)[None, :]).astype(F32)
    tbl = jax.ShapeDtypeStruct((N_EXPERTS, META_LANES), jnp.int32)
    return pl.pallas_call(
        functools.partial(_plan_kernel, tm=tm),
        out_shape=[tbl, tbl, tbl, tbl],
        compiler_params=pltpu.CompilerParams(vmem_limit_bytes=V7X_VMEM_LIMIT),
        name="moe_plan",
    )(gates_t, member, tri)


def _segment_rows(gates, loc_s, tile, strict_upper):
    routed = gates > 0.0
    pos = _dot(routed.astype(BF16), strict_upper).astype(jnp.int32)
    sub = lax.broadcasted_iota(jnp.int32, gates.shape, 0)
    loc = jnp.zeros(gates.shape, jnp.int32)
    for e in range(N_EXPERTS):
        loc = jnp.where(sub == e, loc_s[e, tile], loc)
    rows = loc + pos
    row_a = jnp.min(jnp.where(routed, rows, jnp.iinfo(jnp.int32).max), axis=0, keepdims=True)
    row_b = jnp.max(jnp.where(routed, rows, -1), axis=0, keepdims=True)
    gate_a = jnp.sum(jnp.where(routed & (rows == row_a), gates, 0.0), axis=0, keepdims=True)
    gate_b = jnp.sum(jnp.where(routed & (rows == row_b), gates, 0.0), axis=0, keepdims=True)
    return row_a, row_b, gate_a, gate_b


def _chunk_copy(src, dst, src_row, dst_row, sem):
    aligned = lambda r: r if isinstance(r, int) else pl.multiple_of(r, ROW_ALIGN)
    return pltpu.make_async_copy(src.at[pl.ds(aligned(src_row), ROW_ALIGN)],
                                 dst.at[pl.ds(aligned(dst_row), ROW_ALIGN)], sem)


def _dispatch_kernel(goff_s, loc_s, nch_s, meta_s, gates_ref, h2_ref, upper_ref, xs_hbm,
                     xc_scr, zero_scr, sem, tile_sem):
    i = pl.program_id(0)
    n_ct = pl.num_programs(0)
    row_a, row_b, _, _ = _segment_rows(gates_ref[...], loc_s, i, upper_ref[...])
    r_iota = lax.broadcasted_iota(jnp.int32, (xc_scr.shape[1], gates_ref.shape[1]), 0)
    onehot = jnp.where(r_iota == row_a, 1.0, jnp.where(r_iota == row_b, 1.0, 0.0)).astype(BF16)
    slot = i % 2
    xc_scr[slot] = _dot(onehot, h2_ref[...]).astype(BF16)

    for e in range(N_EXPERTS):
        def issue(c, carry, e=e):
            _chunk_copy(xc_scr.at[slot], xs_hbm, loc_s[e, i] + c * ROW_ALIGN, goff_s[e, i] + c * ROW_ALIGN,
                        sem.at[slot]).start()
            return carry
        lax.fori_loop(0, nch_s[e, i], issue, 0)

    def drain_step(step, buf):
        def drain(c, carry):
            _chunk_copy(xc_scr.at[buf], xs_hbm, 0, 0, sem.at[buf]).wait()
            return carry
        lax.fori_loop(0, sum(nch_s[e, step] for e in range(N_EXPERTS)), drain, 0)

    @pl.when(i > 0)
    def _():
        drain_step(i - 1, 1 - slot)

    @pl.when(i == n_ct - 1)
    def _():
        drain_step(i, slot)
        zero_scr[...] = jnp.zeros(zero_scr.shape, zero_scr.dtype)
        fills = 0
        for e in range(N_EXPERTS):
            def fill(c, carry, e=e):
                _chunk_copy(zero_scr, xs_hbm, 0, meta_s[3, e] + c * ROW_ALIGN, sem.at[0]).start()
                return carry
            lax.fori_loop(0, meta_s[4, e], fill, 0)
            fills = fills + meta_s[4, e]

        def drain_fill(c, carry):
            _chunk_copy(zero_scr, xs_hbm, 0, 0, sem.at[0]).wait()
            return carry
        lax.fori_loop(0, fills, drain_fill, 0)

        tm = zero_scr.shape[0]
        n_mt = xs_hbm.shape[0] // tm

        def tile_copy(t):
            return pltpu.make_async_copy(zero_scr, xs_hbm.at[pl.ds(pl.multiple_of(t * tm, tm), tm)], tile_sem)

        def fill_tile(t, carry):
            tile_copy(t).start()
            return carry
        lax.fori_loop(meta_s[5, 0], n_mt, fill_tile, 0)

        def drain_tile(t, carry):
            tile_copy(t).wait()
            return carry
        lax.fori_loop(meta_s[5, 0], n_mt, drain_tile, 0)


def _moe_dispatch(tables, gates_t, h2, tc, tm):
    T, D = h2.shape
    n_ct, local_rows, n_mt = _moe_dims(T, tc, tm)
    upper = (jnp.arange(tc)[:, None] < jnp.arange(tc)[None, :]).astype(BF16)
    return pl.pallas_call(
        _dispatch_kernel,
        grid_spec=pltpu.PrefetchScalarGridSpec(
            num_scalar_prefetch=4,
            grid=(n_ct,),
            in_specs=[pl.BlockSpec((N_EXPERTS, tc), lambda i, *_: (0, i)),
                      pl.BlockSpec((tc, D), lambda i, *_: (i, 0)),
                      pl.BlockSpec((tc, tc), lambda i, *_: (0, 0))],
            out_specs=pl.BlockSpec(memory_space=pl.ANY),
            scratch_shapes=[pltpu.VMEM((2, local_rows, D), BF16), pltpu.VMEM((tm, D), BF16),
                            pltpu.SemaphoreType.DMA((2,)), pltpu.SemaphoreType.DMA(())]),
        out_shape=jax.ShapeDtypeStruct((n_mt * tm, D), BF16),
        compiler_params=pltpu.CompilerParams(
            dimension_semantics=("arbitrary",), vmem_limit_bytes=V7X_VMEM_LIMIT),
        name="moe_dispatch",
    )(*tables, gates_t, h2, upper)


def _experts_kernel(meta_s, xs_ref, w1_ref, w3_ref, w2_ref, ys_ref, acc_scr):
    m = pl.program_id(0)
    f = pl.program_id(1)

    @pl.when(meta_s[1, m] > 0)
    def _():
        x = xs_ref[...]
        a = _dot(x, w1_ref[0])
        mid = a * _sigmoid(a) * _dot(x, w3_ref[0])
        part = _dot(mid.astype(BF16), w2_ref[0])

        @pl.when(f == 0)
        def _():
            acc_scr[...] = part

        @pl.when(f > 0)
        def _():
            acc_scr[...] += part

        @pl.when(f == pl.num_programs(1) - 1)
        def _():
            ys_ref[...] = acc_scr[...].astype(BF16)

    @pl.when((meta_s[1, m] == 0) & (f == pl.num_programs(1) - 1))
    def _():
        ys_ref[...] = jnp.zeros(ys_ref.shape, ys_ref.dtype)


def _moe_experts(meta, xs, w1, w3, w2, which, tm, tf):
    rows, D = xs.shape
    F = w1.shape[3]
    nf = F // tf
    w1, w3, w2 = (w.reshape((-1,) + w.shape[2:]) for w in (w1, w3, w2))
    meta = meta.at[0].add(which * N_EXPERTS)

    def f_eff(m, f, meta):
        return jnp.where(meta[1, m] > 0, f, nf - 1)

    return pl.pallas_call(
        _experts_kernel,
        grid_spec=pltpu.PrefetchScalarGridSpec(
            num_scalar_prefetch=1,
            grid=(rows // tm, nf),
            in_specs=[pl.BlockSpec((tm, D), lambda m, f, meta: (meta[2, m], 0)),
                      pl.BlockSpec((1, D, tf), lambda m, f, meta: (meta[0, m], 0, f_eff(m, f, meta))),
                      pl.BlockSpec((1, D, tf), lambda m, f, meta: (meta[0, m], 0, f_eff(m, f, meta))),
                      pl.BlockSpec((1, tf, D), lambda m, f, meta: (meta[0, m], f_eff(m, f, meta), 0))],
            out_specs=pl.BlockSpec((tm, D), lambda m, f, meta: (m, 0)),
            scratch_shapes=[pltpu.VMEM((tm, D), F32)]),
        out_shape=jax.ShapeDtypeStruct((rows, D), BF16),
        compiler_params=pltpu.CompilerParams(
            dimension_semantics=("arbitrary", "arbitrary"), vmem_limit_bytes=V7X_VMEM_LIMIT),
        name="moe_experts",
    )(meta, xs, w1, w3, w2)


def _combine_kernel(goff_s, loc_s, nch_s, gates_ref, x1_ref, upper_ref, ys_hbm, o_ref, yc_scr, sem):
    i = pl.program_id(0)
    tc = gates_ref.shape[1]

    slot = i % 2

    def fetch(step, buf):
        for e in range(N_EXPERTS):
            def issue(c, carry, e=e):
                _chunk_copy(ys_hbm, yc_scr.at[buf], goff_s[e, step] + c * ROW_ALIGN,
                            loc_s[e, step] + c * ROW_ALIGN, sem.at[buf]).start()
                return carry
            lax.fori_loop(0, nch_s[e, step], issue, 0)

    @pl.when(i == 0)
    def _():
        yc_scr[...] = jnp.zeros(yc_scr.shape, yc_scr.dtype)
        fetch(i, slot)

    @pl.when(i + 1 < pl.num_programs(0))
    def _():
        fetch(i + 1, 1 - slot)

    row_a, row_b, gate_a, gate_b = _segment_rows(gates_ref[...], loc_s, i, upper_ref[...])
    stats = jnp.concatenate([row_a.astype(F32), row_b.astype(F32), gate_a, gate_b,
                             jnp.zeros((META_LANES - 4, tc), F32)], axis=0).T
    lane = lax.broadcasted_iota(jnp.int32, (tc, yc_scr.shape[1]), 1).astype(F32)
    sel = jnp.where(lane == stats[:, 0:1], stats[:, 2:3],
                    jnp.where(lane == stats[:, 1:2], stats[:, 3:4], 0.0)).astype(BF16)

    def drain(c, carry):
        _chunk_copy(ys_hbm, yc_scr.at[slot], 0, 0, sem.at[slot]).wait()
        return carry
    lax.fori_loop(0, sum(nch_s[e, i] for e in range(N_EXPERTS)), drain, 0)
    o_ref[...] = x1_ref[...] + _dot(sel, yc_scr[slot])


def _moe_combine(tables, gates_t, x1, ys, tc, tm):
    T, D = x1.shape
    n_ct, local_rows, _ = _moe_dims(T, tc, tm)
    upper = (jnp.arange(tc)[:, None] < jnp.arange(tc)[None, :]).astype(BF16)
    return pl.pallas_call(
        _combine_kernel,
        grid_spec=pltpu.PrefetchScalarGridSpec(
            num_scalar_prefetch=3,
            grid=(n_ct,),
            in_specs=[pl.BlockSpec((N_EXPERTS, tc), lambda i, *_: (0, i)),
                      pl.BlockSpec((tc, D), lambda i, *_: (i, 0)),
                      pl.BlockSpec((tc, tc), lambda i, *_: (0, 0)),
                      pl.BlockSpec(memory_space=pl.ANY)],
            out_specs=pl.BlockSpec((tc, D), lambda i, *_: (i, 0)),
            scratch_shapes=[pltpu.VMEM((2, local_rows, D), BF16), pltpu.SemaphoreType.DMA((2,))]),
        out_shape=jax.ShapeDtypeStruct((T, D), F32),
        compiler_params=pltpu.CompilerParams(
            dimension_semantics=("arbitrary",), vmem_limit_bytes=V7X_VMEM_LIMIT),
        name="moe_combine",
    )(*tables[:3], gates_t, x1, upper, ys)


def _moe(x1, h2, gates_t, w1, w3, w2, which, tc, tm, tf):
    goff, loc, nch, meta = _moe_plan(gates_t, tc, tm)
    xs = _moe_dispatch((goff, loc, nch, meta), gates_t, h2, tc, tm)
    ys = _moe_experts(meta, xs, w1, w3, w2, which, tm, tf)
    return _moe_combine((goff, loc, nch), gates_t, x1, ys, tc, tm)


def _rope_tables(seq):
    pos = jnp.arange(seq, dtype=F32)
    inv = 1.0 / (ROPE_THETA ** (jnp.arange(0, QK_DIM, 2, dtype=F32) / QK_DIM))
    ang = pos[:, None] * inv[None, :]
    cos, sin = jnp.cos(ang), jnp.sin(ang)
    groups = BR // QK_DIM
    cos_t = jnp.tile(jnp.concatenate([cos, cos], axis=-1), (1, groups))
    sin_t = jnp.tile(jnp.concatenate([-sin, sin], axis=-1), (1, groups))
    return cos_t, sin_t


def _block_diag(blocks):
    g, n, m = blocks.shape
    eye = jnp.eye(g, dtype=blocks.dtype)
    return (eye[:, None, :, None] * blocks[:, :, None, :]).reshape(g * n, g * m)


def kernel(x, norm1_g, w_in, w_gate, b_gate, w_branch, w_out, pool_w, pool_scale, attn_qn_g, attn_kn_g,
           lam_q1, lam_k1, lam_q2, lam_k2, attn_subln_g, conv_w, conv_b, conv_ln_g, conv_ln_b,
           sgu_ln_g, sgu_ln_b, sgu_w, sgu_b, norm2_g, ffn_w1, ffn_w3, ffn_w2, moe_router,
           moe_w1, moe_w3, moe_w2):
    B, S, D = x.shape
    depth = w_in.shape[0]
    T = B * S
    tm_mix = min(512, S)
    tq = min(256, S)
    tm_merge = min(512, T)
    tm_ffn = min(1024, T)
    tm_moe = 512

    cos_t, sin_t = _rope_tables(S)
    bd32 = _block_diag(jnp.ones((BR // QK_DIM, QK_DIM, QK_DIM), BF16))
    row = lambda v: v.reshape(1, -1).astype(F32)

    stacked = {"w_in": w_in, "w_gate": w_gate, "w_branch": w_branch, "w_out": w_out, "ffn_w1": ffn_w1,
               "ffn_w3": ffn_w3, "ffn_w2": ffn_w2, "moe_w1": moe_w1, "moe_w3": moe_w3, "moe_w2": moe_w2}
    first_in = w_in[0:1].astype(BF16)
    bf = {}

    for l in range(depth):
        x3 = x.reshape(B, S, D)
        q, k, vt, ya, yc, yd = _mixer_in(
            x3, row(norm1_g[l]), first_in if l == 0 else bf["w_in"], 0 if l == 0 else l, cos_t, sin_t,
            row(jnp.tile(attn_qn_g[l], BR // QK_DIM)), row(jnp.tile(attn_kn_g[l], BR // QK_DIM)), bd32,
            _block_diag(pool_w[l]).astype(BF16), row(pool_scale[l]),
            conv_w[l], row(conv_b[l]), row(conv_ln_g[l]), row(conv_ln_b[l]),
            row(sgu_ln_g[l]), row(sgu_ln_b[l]), sgu_w[l],
            jnp.repeat(sgu_b[l].T, BR // SGU_GROUPS, axis=1), tm_mix)
        lamv = jnp.stack([lam_q1[l], lam_k1[l], lam_q2[l], lam_k2[l]]).astype(F32)
        lam_init = 0.8 - 0.6 * math.exp(-0.3 * l)
        sg_col = jnp.broadcast_to(attn_subln_g[l].astype(F32)[:, None], (V_DIM, tq))
        score_bound = (QK_DIM ** 0.5 * LOG2E) * jnp.max(jnp.abs(attn_qn_g[l])) * jnp.max(jnp.abs(attn_kn_g[l]))
        bounded = (score_bound <= SAFE_SCORE_BOUND).astype(jnp.int32).reshape(1)
        yb, cast = _attention(bounded, lamv, q, k, vt, sg_col, lam_init, tq,
                              side=list(stacked.values()) if l == 0 else ())
        if l == 0:
            bf = dict(zip(stacked, cast))

        is_moe = l % 2 == 1
        router_t = moe_router[l // 2].T.astype(F32) if is_moe else None
        flat = lambda y: y.reshape(T, BR)
        outs = _merge(x.reshape(T, D), flat(ya), flat(yb), flat(yc), flat(yd), row(norm1_g[l]),
                      bf["w_gate"], row(b_gate[l]), bf["w_branch"], bf["w_out"], l, row(norm2_g[l]),
                      router_t, tm_merge)
        i = l // 2
        if is_moe:
            x1, h2, gates_t = outs
            x = _moe(x1, h2, gates_t, bf["moe_w1"], bf["moe_w3"], bf["moe_w2"], i,
                     tm_merge, tm_moe, moe_w1.shape[3] // 2)
        else:
            x1, h2 = outs
            x = _ffn(x1, h2, bf["ffn_w1"], bf["ffn_w3"], bf["ffn_w2"], i, tm_ffn, FFN_CHUNK)
    return x.reshape(B, S, D)
```

```python
import functools
import math

import jax
import jax.numpy as jnp
from jax import lax
from jax.experimental import pallas as pl
from jax.experimental.pallas import tpu as pltpu

F32 = jnp.float32
BF16 = jnp.bfloat16

N_BRANCH = 4
BR = 256
POOL_WINDOWS = (2, 4, 8, 16)
POOL_GW = 64
HEADS = 4
QK_DIM = 32
V_DIM = 64
ROPE_THETA = 10000.0
CHUNK = 64
CONV_W = 31
SGU_LEN = 128
SGU_GROUPS = 4
N_EXPERTS = 8
EPS = 1e-6
QK_GROUPS = 2 * HEADS
LOG2E = math.log2(math.e)
FFN_CHUNK = 1024
KV_TILES_PER_STEP = 4
SAFE_SCORE_BOUND = 40.0
NEG_INF = -1e30

HALO = 32
V7X_VMEM_LIMIT = 56 * 1024 * 1024


def _rms(x, g):
    return x * lax.rsqrt(jnp.mean(x * x, axis=-1, keepdims=True) + EPS) * g


def _ln(x, g, b):
    mu = jnp.mean(x, axis=-1, keepdims=True)
    xc = x - mu
    return xc * lax.rsqrt(jnp.mean(xc * xc, axis=-1, keepdims=True) + EPS) * g + b


def _sigmoid(x):
    return 1.0 / (1.0 + jnp.exp(-x))


def _dot(a, b):
    return jnp.dot(a, b, preferred_element_type=F32)


def _mixer_in_kernel(x_ref, xh_ref, g1_ref, win_ref, cos_ref, sin_ref, qg_ref, kg_ref, bd32_ref,
                     poolw_ref, pools_ref, convw_ref, convb_ref, clng_ref, clnb_ref,
                     slng_ref, slnb_ref, sguw_ref, sgub_ref,
                     q_ref, k_ref, vt_ref, ya_ref, yc_ref, yd_ref,
                     a_scr, s2_scr, s4_scr, s8_scr, u_scr, us_scr):
    i = pl.program_id(1)
    tm = x_ref.shape[1]
    g1 = g1_ref[...]
    h = _rms(x_ref[0], g1).astype(BF16)
    proj = _dot(h, win_ref[0])

    hh = _rms(xh_ref[0], g1).astype(BF16)
    halo_ok = (i > 0).astype(F32)
    pa_h = _dot(hh, win_ref[0, :, 0:BR]) * halo_ok
    pc_h = _dot(hh, win_ref[0, :, 4 * BR:6 * BR]) * halo_ok

    a = proj[:, 0:BR]
    a_scr[0:HALO, :] = pa_h
    a_scr[HALO:HALO + tm, :] = a
    n_ext = tm + HALO
    s2_scr[8:n_ext, :] = a_scr[8:n_ext, :] + a_scr[7:n_ext - 1, :]
    s4_scr[16:n_ext, :] = s2_scr[16:n_ext, :] + s2_scr[14:n_ext - 2, :]
    s8_scr[24:n_ext, :] = s4_scr[24:n_ext, :] + s4_scr[20:n_ext - 4, :]
    s16 = s8_scr[HALO:n_ext, :] + s8_scr[HALO - 8:n_ext - 8, :]
    lane = lax.broadcasted_iota(jnp.int32, (tm, BR), 1)
    row = lax.broadcasted_iota(jnp.int32, (tm, BR), 0)
    win_sum = jnp.where(lane < POOL_GW, s2_scr[HALO:n_ext, :],
                        jnp.where(lane < 2 * POOL_GW, s4_scr[HALO:n_ext, :],
                                  jnp.where(lane < 3 * POOL_GW, s8_scr[HALO:n_ext, :], s16)))
    width = jnp.where(lane < POOL_GW, POOL_WINDOWS[0],
                      jnp.where(lane < 2 * POOL_GW, POOL_WINDOWS[1],
                                jnp.where(lane < 3 * POOL_GW, POOL_WINDOWS[2], POOL_WINDOWS[3])))
    cnt = jnp.minimum(width, i * tm + row + 1).astype(F32)
    pooled = win_sum / cnt - a
    ya_ref[0] = (_dot(pooled.astype(BF16), poolw_ref[...]) * pools_ref[...]).astype(BF16)

    lo_half = (lane % QK_DIM) < (QK_DIM // 2)

    def qk_norm_rope(t, g_ref, scale):
        ss = _dot((t * t).astype(BF16), bd32_ref[...])
        tn = t * lax.rsqrt(ss * (1.0 / QK_DIM) + EPS) * g_ref[...]
        rot = jnp.where(lo_half, pltpu.roll(tn, BR - QK_DIM // 2, 1), pltpu.roll(tn, QK_DIM // 2, 1))
        return (tn * cos_ref[...] + rot * sin_ref[...]) * scale

    q_ref[0] = qk_norm_rope(proj[:, BR:2 * BR], qg_ref, QK_DIM ** -0.5 * LOG2E).T.astype(BF16)
    kk = qk_norm_rope(proj[:, 2 * BR:3 * BR], kg_ref, 1.0).astype(BF16)
    for g in range(QK_GROUPS):
        k_ref[0, g] = kk[:, g * QK_DIM:(g + 1) * QK_DIM]
    vt_ref[0] = proj[:, 3 * BR:4 * BR].T.astype(BF16)

    u_scr[0:HALO, :] = pc_h[:, 0:BR] * _sigmoid(pc_h[:, BR:2 * BR])
    u_scr[HALO:HALO + tm, :] = proj[:, 4 * BR:5 * BR] * _sigmoid(proj[:, 5 * BR:6 * BR])
    taps = [convw_ref[j:j + 1, :] for j in range(CONV_W)]
    convb = convb_ref[...]
    clng = clng_ref[...]
    clnb = clnb_ref[...]
    n_sh = us_scr.shape[1]
    for s in range(1, 8):
        us_scr[s - 1] = u_scr[s:s + n_sh, :]
    rc = 64
    base = HALO - (CONV_W - 1)

    def tap_window(c, j):
        s, r0 = (base + j) % 8, c * rc + (base + j) // 8 * 8
        return u_scr[r0:r0 + rc, :] if s == 0 else us_scr[s - 1, r0:r0 + rc, :]

    for c in range(tm // rc):
        acc = convb + taps[0] * tap_window(c, 0)
        for j in range(1, CONV_W):
            acc = acc + taps[j] * tap_window(c, j)
        y = _ln(acc, clng, clnb)
        yc_ref[0, c * rc:(c + 1) * rc, :] = (y * _sigmoid(y)).astype(BF16)

    d_in = proj[:, 6 * BR:8 * BR]
    z = 0.5 * d_in * (1.0 + lax.erf(d_in * (2.0 ** -0.5)))
    vv = _ln(z[:, BR:2 * BR], slng_ref[...], slnb_ref[...]).astype(BF16)
    tr = lax.broadcasted_iota(jnp.int32, (SGU_LEN, SGU_LEN), 0)
    tc = lax.broadcasted_iota(jnp.int32, (SGU_LEN, SGU_LEN), 1)
    wtri = [jnp.where(tr >= tc, sguw_ref[g], 0.0).astype(BF16) for g in range(SGU_GROUPS)]
    lane_g = lax.broadcasted_iota(jnp.int32, (SGU_LEN, BR), 1) // (BR // SGU_GROUPS)
    sgub = sgub_ref[...]
    for n in range(tm // SGU_LEN):
        vn = vv[n * SGU_LEN:(n + 1) * SGU_LEN, :]
        sv = _dot(wtri[0], vn)
        for g in range(1, SGU_GROUPS):
            sv = jnp.where(lane_g == g, _dot(wtri[g], vn), sv)
        yd_ref[0, n * SGU_LEN:(n + 1) * SGU_LEN, :] = (
            z[n * SGU_LEN:(n + 1) * SGU_LEN, 0:BR] * (sv + sgub)).astype(BF16)


def _mixer_in(x3, g1, win, win_l, cos_t, sin_t, qg, kg, bd32, poolw, pools, convw, convb, clng, clnb,
              slng, slnb, sguw, sgub, tm):
    B, S, D = x3.shape
    nt = S // tm
    hb = tm // HALO
    full = lambda *shape: pl.BlockSpec(shape, lambda b, i: (0,) * len(shape))
    in_specs = [
        pl.BlockSpec((1, tm, D), lambda b, i: (b, i, 0)),
        pl.BlockSpec((1, HALO, D), lambda b, i: (b, jnp.maximum(i * hb - 1, 0), 0)),
        full(1, D), pl.BlockSpec((1, D, 8 * BR), lambda b, i: (win_l, 0, 0)),
        pl.BlockSpec((tm, BR), lambda b, i: (i, 0)),
        pl.BlockSpec((tm, BR), lambda b, i: (i, 0)),
        full(1, BR), full(1, BR), full(BR, BR),
        full(BR, BR), full(1, BR), full(CONV_W, BR), full(1, BR), full(1, BR), full(1, BR),
        full(1, BR), full(1, BR), full(SGU_GROUPS, SGU_LEN, SGU_LEN), full(SGU_LEN, BR),
    ]
    tok = lambda: pl.BlockSpec((1, tm, BR), lambda b, i: (b, i, 0))
    split = lambda: pl.BlockSpec((1, QK_GROUPS, tm, QK_DIM), lambda b, i: (b, 0, i, 0))
    flipped = lambda: pl.BlockSpec((1, BR, tm), lambda b, i: (b, 0, i))
    out_specs = [
        flipped(), split(), flipped(),
        tok(), tok(), tok(),
    ]
    out_shape = [
        jax.ShapeDtypeStruct((B, BR, S), BF16),
        jax.ShapeDtypeStruct((B, QK_GROUPS, S, QK_DIM), BF16),
        jax.ShapeDtypeStruct((B, BR, S), BF16),
        jax.ShapeDtypeStruct((B, S, BR), BF16),
        jax.ShapeDtypeStruct((B, S, BR), BF16),
        jax.ShapeDtypeStruct((B, S, BR), BF16),
    ]
    ext = pltpu.VMEM((tm + HALO, BR), F32)
    return pl.pallas_call(
        _mixer_in_kernel,
        grid=(B, nt),
        in_specs=in_specs,
        out_specs=out_specs,
        out_shape=out_shape,
        scratch_shapes=[ext, ext, ext, ext, ext, pltpu.VMEM((7, tm + HALO - 8, BR), F32)],
        compiler_params=pltpu.CompilerParams(
            dimension_semantics=("parallel", "parallel"), vmem_limit_bytes=V7X_VMEM_LIMIT),
        name="mixer_in",
    )(x3, x3, g1, win, cos_t, sin_t, qg, kg, bd32, poolw, pools, convw, convb, clng, clnb,
      slng, slnb, sguw, sgub)


def _attn_kernel(bounded_s, lam_ref, q_ref, k_ref, vt_ref, sg_ref, *rest, lam_init, tq, n_side):
    o_ref, p_scr = rest[n_side], rest[-1]
    _cast_side_blocks(rest[:n_side], rest[n_side + 1:2 * n_side + 1])
    i = pl.program_id(1)
    key_chunk = lax.broadcasted_iota(jnp.int32, (tq, tq), 0) // CHUNK
    qry_chunk = lax.broadcasted_iota(jnp.int32, (tq, tq), 1) // CHUNK
    diag_mask = key_chunk <= qry_chunk

    def scores(g, j, masked):
        k0 = pl.multiple_of(j * tq, tq)
        st = _dot(k_ref[0, g, pl.ds(k0, tq), :], q_ref[0, g * QK_DIM:(g + 1) * QK_DIM, :])
        return jnp.where(diag_mask, st, NEG_INF) if masked else st

    def fold8(x, op):
        return op(x.reshape(tq // 8, 8, tq), axis=0)

    def exact_maxima():
        def max_tile(j, ms, masked):
            return tuple(jnp.maximum(ms[g], fold8(scores(g, j, masked), jnp.max)) for g in range(QK_GROUPS))

        ms = tuple(jnp.full((8, tq), NEG_INF, F32) for _ in range(QK_GROUPS))
        ms = lax.fori_loop(0, i, functools.partial(max_tile, masked=False), ms)
        ms = max_tile(i, ms, True)
        return [jnp.max(x, axis=0, keepdims=True) for x in ms]

    def softmax_pv(shift):
        def acc_tile(j, carry, masked, buf=0):
            k0 = pl.multiple_of(j * tq, tq)
            ls = []
            for g in range(QK_GROUPS):
                st = scores(g, j, masked)
                p = jnp.exp2(st if shift is None else st - shift[g])
                p_scr[buf, g] = p.astype(BF16)
                ls.append(carry[g][0] + fold8(p, jnp.sum))
            out = []
            for g in range(QK_GROUPS):
                hd = g // 2
                vt = vt_ref[0, hd * V_DIM:(hd + 1) * V_DIM, pl.ds(k0, tq)]
                out.append((ls[g], carry[g][1] + _dot(vt, p_scr[buf, g])))
            return tuple(out)

        n_buf = p_scr.shape[0]

        def acc_group(jj, carry):
            for t in range(n_buf):
                carry = acc_tile(n_buf * jj + t, carry, False, t)
            return carry

        init = tuple((jnp.zeros((8, tq), F32), jnp.zeros((V_DIM, tq), F32)) for _ in range(QK_GROUPS))
        carry = lax.fori_loop(0, i // n_buf, acc_group, init)
        carry = lax.fori_loop(i - i % n_buf, i, functools.partial(acc_tile, masked=False), carry)
        carry = acc_tile(i, carry, True)
        return [(jnp.sum(l8, axis=0, keepdims=True), acc) for l8, acc in carry]

    def finish(res):
        lamv = lam_ref[...]
        lam = (jnp.exp(jnp.sum(lamv[0:1] * lamv[1:2], axis=-1, keepdims=True))
               - jnp.exp(jnp.sum(lamv[2:3] * lamv[3:4], axis=-1, keepdims=True)) + lam_init)
        sg = sg_ref[...]
        heads = []
        for hd in range(HEADS):
            o = res[2 * hd][1] / res[2 * hd][0] - lam * (res[2 * hd + 1][1] / res[2 * hd + 1][0])
            o = o * lax.rsqrt(jnp.mean(o * o, axis=0, keepdims=True) + EPS) * sg * (1.0 - lam_init)
            heads.append(o)
        o_ref[0] = jnp.concatenate(heads, axis=0).T.astype(BF16)

    @pl.when(bounded_s[0] == 1)
    def _():
        finish(softmax_pv(None))

    @pl.when(bounded_s[0] != 1)
    def _():
        finish(softmax_pv(exact_maxima()))


def _cast_rows_per_block(rows, steps):
    for rpb in range(ROW_ALIGN, rows + 1, ROW_ALIGN):
        if rows % rpb == 0 and rows // rpb <= steps:
            return rpb
    raise ValueError(f"no aligned row block for {rows} rows in {steps} steps")


def _side_cast_specs(side, steps, step_of):
    side2d = [w.reshape(-1, w.shape[-1]) for w in side]
    specs = []
    for w in side2d:
        rpb = _cast_rows_per_block(w.shape[0], steps)
        last = w.shape[0] // rpb - 1
        specs.append(pl.BlockSpec(
            (rpb, w.shape[1]), lambda *idx, last=last: (jnp.minimum(step_of(*idx), last), 0)))
    return side2d, specs


def _cast_side_blocks(in_refs, out_refs):
    for w_ref, wb_ref in zip(in_refs, out_refs):
        wb_ref[...] = w_ref[...].astype(BF16)


def _attention(bounded, lamv, q, k, vt, sg_col, lam_init, tq, side=()):
    B, _, S = q.shape
    nq = S // tq
    side2d, side_specs = _side_cast_specs(side, B * nq, lambda b, i, *_: b * nq + i)
    outs = pl.pallas_call(
        functools.partial(_attn_kernel, lam_init=lam_init, tq=tq, n_side=len(side2d)),
        grid_spec=pltpu.PrefetchScalarGridSpec(
            num_scalar_prefetch=1,
            grid=(B, nq),
            in_specs=[
                pl.BlockSpec((4, QK_DIM), lambda b, i, *_: (0, 0)),
                pl.BlockSpec((1, BR, tq), lambda b, i, *_: (b, 0, i)),
                pl.BlockSpec((1, QK_GROUPS, S, QK_DIM), lambda b, i, *_: (b, 0, 0, 0)),
                pl.BlockSpec((1, BR, S), lambda b, i, *_: (b, 0, 0)),
                pl.BlockSpec((V_DIM, tq), lambda b, i, *_: (0, 0)),
            ] + side_specs,
            out_specs=[pl.BlockSpec((1, tq, BR), lambda b, i, *_: (b, i, 0))] + side_specs,
            scratch_shapes=[pltpu.VMEM((KV_TILES_PER_STEP, QK_GROUPS, tq, tq), BF16)]),
        out_shape=[jax.ShapeDtypeStruct((B, S, BR), BF16)]
                  + [jax.ShapeDtypeStruct(w.shape, BF16) for w in side2d],
        compiler_params=pltpu.CompilerParams(
            dimension_semantics=("arbitrary", "arbitrary"), vmem_limit_bytes=V7X_VMEM_LIMIT),
        name="diff_attn",
    )(bounded, lamv, q, k, vt, sg_col, *side2d)
    return outs[0], [o.reshape(w.shape) for o, w in zip(outs[1:], side)]


def _merge_kernel(*refs, with_router, n_side):
    n_in = 12 if with_router else 11
    (x_ref, ya_ref, yb_ref, yc_ref, yd_ref, g1_ref, wg_ref, bg_ref, wb_ref, wo_ref, g2_ref) = refs[:11]
    rt_ref = refs[11] if with_router else None
    outs = refs[n_in + n_side:]
    x1_ref, h2_ref = outs[0], outs[1]
    gates_ref = outs[2] if with_router else None
    _cast_side_blocks(refs[n_in:n_in + n_side], outs[len(outs) - n_side:])
    x = x_ref[...]
    d = x.shape[1]
    h = _rms(x, g1_ref[...]).astype(BF16)
    merged = None
    for b, y_ref in enumerate((ya_ref, yb_ref, yc_ref, yd_ref)):
        gate = _sigmoid(_dot(h, wg_ref[0, :, b * d:(b + 1) * d]) + bg_ref[:, b * d:(b + 1) * d])
        term = gate * _dot(y_ref[...], wb_ref[0, b])
        merged = term if merged is None else merged + term
    x1 = x + _dot(merged.astype(BF16), wo_ref[0])
    x1_ref[...] = x1
    h2 = _rms(x1, g2_ref[...])
    h2_ref[...] = h2.astype(BF16)
    if with_router:
        def split(t):
            hi = t.astype(BF16)
            return hi, (t - hi.astype(F32)).astype(BF16)

        def nt_dot(a, b):
            return lax.dot_general(a, b, (((1,), (1,)), ((), ())), preferred_element_type=F32)

        (r_hi, r_lo), (h_hi, h_lo) = split(rt_ref[...]), split(h2)
        logits = nt_dot(r_hi, h_hi) + (nt_dot(r_hi, h_lo) + nt_dot(r_lo, h_hi))
        e_idx = lax.broadcasted_iota(jnp.int32, logits.shape, 0)
        v1 = jnp.max(logits, axis=0, keepdims=True)
        i1 = jnp.min(jnp.where(logits == v1, e_idx, N_EXPERTS), axis=0, keepdims=True)
        rest = jnp.where(e_idx == i1, -jnp.inf, logits)
        v2 = jnp.max(rest, axis=0, keepdims=True)
        i2 = jnp.min(jnp.where(rest == v2, e_idx, N_EXPERTS), axis=0, keepdims=True)
        w2 = 1.0 / (1.0 + jnp.exp(v1 - v2))
        gates_ref[...] = jnp.where(e_idx == i1, 1.0 - w2, jnp.where(e_idx == i2, w2, 0.0))


def _merge(x2, ya, yb, yc, yd, g1, wg, bg, wb, wo, layer, g2, router_t, tm, side=()):
    T, D = x2.shape
    with_router = router_t is not None
    side2d, side_specs = _side_cast_specs(side, T // tm, lambda i: i)
    full = lambda *shape: pl.BlockSpec(shape, lambda i: (0,) * len(shape))
    of_layer = lambda *shape: pl.BlockSpec((1,) + shape, lambda i: (layer,) + (0,) * len(shape))
    tokb = lambda w: pl.BlockSpec((tm, w), lambda i: (i, 0))
    in_specs = [tokb(D), tokb(BR), tokb(BR), tokb(BR), tokb(BR),
                full(1, D), of_layer(D, N_BRANCH * D), full(1, N_BRANCH * D), of_layer(N_BRANCH, BR, D),
                of_layer(D, D), full(1, D)]
    args = [x2, ya, yb, yc, yd, g1, wg, bg, wb, wo, g2]
    out_specs = [tokb(D), tokb(D)]
    out_shape = [jax.ShapeDtypeStruct((T, D), F32), jax.ShapeDtypeStruct((T, D), BF16)]
    if with_router:
        in_specs.append(full(N_EXPERTS, D))
        args.append(router_t)
        out_specs.append(pl.BlockSpec((N_EXPERTS, tm), lambda i: (0, i)))
        out_shape.append(jax.ShapeDtypeStruct((N_EXPERTS, T), F32))
    outs = pl.pallas_call(
        functools.partial(_merge_kernel, with_router=with_router, n_side=len(side2d)),
        grid=(T // tm,),
        in_specs=in_specs + side_specs,
        out_specs=out_specs + side_specs,
        out_shape=out_shape + [jax.ShapeDtypeStruct(w.shape, BF16) for w in side2d],
        compiler_params=pltpu.CompilerParams(
            dimension_semantics=("arbitrary",), vmem_limit_bytes=V7X_VMEM_LIMIT),
        name="merge_router" if with_router else "merge",
    )(*args, *side2d)
    n_main = len(out_shape)
    return outs[:n_main], [o.reshape(w.shape) for o, w in zip(outs[n_main:], side)]


def _ffn_kernel(x1_ref, h2_ref, w1_ref, w3_ref, w2_ref, o_ref, *, tf):
    h2 = h2_ref[...]
    acc = x1_ref[...]
    for c0 in range(0, w1_ref.shape[2], tf):
        c1 = min(c0 + tf, w1_ref.shape[2])
        a = _dot(h2, w1_ref[0, :, c0:c1])
        mid = a * _sigmoid(a) * _dot(h2, w3_ref[0, :, c0:c1])
        acc = acc + _dot(mid.astype(BF16), w2_ref[0, c0:c1, :])
    o_ref[...] = acc


def _ffn(x1, h2, w1, w3, w2, which, tm, tf):
    T, D = x1.shape
    F = w1.shape[2]
    resident = lambda *shape: pl.BlockSpec((1,) + shape, lambda i: (which, 0, 0), pipeline_mode=pl.Buffered(1))
    return pl.pallas_call(
        functools.partial(_ffn_kernel, tf=tf),
        grid=(T // tm,),
        in_specs=[pl.BlockSpec((tm, D), lambda i: (i, 0)),
                  pl.BlockSpec((tm, D), lambda i: (i, 0)),
                  resident(D, F), resident(D, F), resident(F, D)],
        out_specs=pl.BlockSpec((tm, D), lambda i: (i, 0)),
        out_shape=jax.ShapeDtypeStruct((T, D), F32),
        compiler_params=pltpu.CompilerParams(
            dimension_semantics=("parallel",), vmem_limit_bytes=V7X_VMEM_LIMIT),
        name="dense_ffn",
    )(x1, h2, w1, w3, w2)


ROW_ALIGN = 16
META_LANES = 128


def _moe_dims(T, tc, tm):
    n_ct = T // tc
    local_rows = 2 * tc + N_EXPERTS * ROW_ALIGN
    max_rows = 2 * T + n_ct * N_EXPERTS * (ROW_ALIGN - 1)
    n_mt = -(-max_rows // tm) + N_EXPERTS
    assert n_ct <= META_LANES and n_mt <= META_LANES and tm & (tm - 1) == 0
    return n_ct, local_rows, n_mt


def _sublane_excl_scan(x):
    rows, run = [], jnp.zeros_like(x[0:1, :])
    for e in range(x.shape[0]):
        rows.append(run)
        run = run + x[e:e + 1, :]
    return jnp.concatenate(rows, axis=0)


def _plan_kernel(gates_ref, member_ref, tri_ref, goff_ref, loc_ref, nch_ref, meta_ref, *, tm):
    routed = (gates_ref[...] > 0.0).astype(BF16)
    cnt = _dot(routed, member_ref[...]).astype(jnp.int32)
    row_sh, tm_sh = ROW_ALIGN.bit_length() - 1, tm.bit_length() - 1
    pad = ((cnt + (ROW_ALIGN - 1)) >> row_sh) << row_sh
    incl = jnp.dot(pad.astype(F32), tri_ref[...], preferred_element_type=F32,
                   precision=lax.Precision.HIGHEST).astype(jnp.int32)
    tot = jnp.max(incl, axis=1, keepdims=True)
    etot = ((tot + (tm - 1)) >> tm_sh) << tm_sh
    base = _sublane_excl_scan(etot)
    goff_ref[...] = base + incl - pad
    loc_ref[...] = _sublane_excl_scan(pad)
    nch_ref[...] = pad >> row_sh

    lane = lax.broadcasted_iota(jnp.int32, (N_EXPERTS, META_LANES), 1)
    sub = lax.broadcasted_iota(jnp.int32, (N_EXPERTS, META_LANES), 0)
    end = base + etot
    n_used = jnp.max(end, axis=0, keepdims=True) >> tm_sh
    src = jnp.minimum(lane, n_used - 1)
    expert = jnp.minimum(jnp.sum((src * tm >= end).astype(jnp.int32), axis=0, keepdims=True), N_EXPERTS - 1)
    start = lane * tm
    inside = (start >= base) & (start < end)
    valid = jnp.sum(jnp.where(inside, jnp.clip(tot - (start - base), 0, tm), 0), axis=0, keepdims=True)
    fill_start = jnp.sum(jnp.where(sub == lane, base + tot, 0), axis=0, keepdims=True)
    fill_nch = jnp.sum(jnp.where(sub == lane, (etot - tot) >> row_sh, 0), axis=0, keepdims=True)
    meta = jnp.where(sub == 0, expert, jnp.where(sub == 1, valid, jnp.where(sub == 2, src,
                     jnp.where(sub == 3, fill_start, jnp.where(sub == 4, fill_nch,
                     jnp.where(sub == 5, n_used, 0))))))
    meta_ref[...] = meta


def _moe_plan(gates_t, tc, tm):
    E, T = gates_t.shape
    member = (jnp.arange(T)[:, None] // tc == jnp.arange(META_LANES)[None, :]).astype(BF16)
    tri = (jnp.arange(META_LANES)[:, None] <= jnp.arange(META_LANES)[None, :]).astype(F32)
    tbl = jax.ShapeDtypeStruct((N_EXPERTS, META_LANES), jnp.int32)
    return pl.pallas_call(
        functools.partial(_plan_kernel, tm=tm),
        out_shape=[tbl, tbl, tbl, tbl],
        compiler_params=pltpu.CompilerParams(vmem_limit_bytes=V7X_VMEM_LIMIT),
        name="moe_plan",
    )(gates_t, member, tri)


def _segment_rows(gates, loc_s, tile, strict_upper):
    routed = gates > 0.0
    pos = _dot(routed.astype(BF16), strict_upper).astype(jnp.int32)
    sub = lax.broadcasted_iota(jnp.int32, gates.shape, 0)
    loc = jnp.zeros(gates.shape, jnp.int32)
    for e in range(N_EXPERTS):
        loc = jnp.where(sub == e, loc_s[e, tile], loc)
    rows = loc + pos
    row_a = jnp.min(jnp.where(routed, rows, jnp.iinfo(jnp.int32).max), axis=0, keepdims=True)
    row_b = jnp.max(jnp.where(routed, rows, -1), axis=0, keepdims=True)
    gate_a = jnp.sum(jnp.where(routed & (rows == row_a), gates, 0.0), axis=0, keepdims=True)
    gate_b = jnp.sum(jnp.where(routed & (rows == row_b), gates, 0.0), axis=0, keepdims=True)
    return row_a, row_b, gate_a, gate_b


def _chunk_copy(src, dst, src_row, dst_row, sem):
    aligned = lambda r: r if isinstance(r, int) else pl.multiple_of(r, ROW_ALIGN)
    return pltpu.make_async_copy(src.at[pl.ds(aligned(src_row), ROW_ALIGN)],
                                 dst.at[pl.ds(aligned(dst_row), ROW_ALIGN)], sem)


def _dispatch_kernel(goff_s, loc_s, nch_s, meta_s, gates_ref, h2_ref, upper_ref, xs_hbm,
                     xc_scr, zero_scr, sem, tile_sem):
    i = pl.program_id(0)
    n_ct = pl.num_programs(0)
    row_a, row_b, _, _ = _segment_rows(gates_ref[...], loc_s, i, upper_ref[...])
    r_iota = lax.broadcasted_iota(jnp.int32, (xc_scr.shape[1], gates_ref.shape[1]), 0)
    onehot = jnp.where(r_iota == row_a, 1.0, jnp.where(r_iota == row_b, 1.0, 0.0)).astype(BF16)
    slot = i % 2
    xc_scr[slot] = _dot(onehot, h2_ref[...]).astype(BF16)

    for e in range(N_EXPERTS):
        def issue(c, carry, e=e):
            _chunk_copy(xc_scr.at[slot], xs_hbm, loc_s[e, i] + c * ROW_ALIGN, goff_s[e, i] + c * ROW_ALIGN,
                        sem.at[slot]).start()
            return carry
        lax.fori_loop(0, nch_s[e, i], issue, 0)

    def drain_step(step, buf):
        def drain(c, carry):
            _chunk_copy(xc_scr.at[buf], xs_hbm, 0, 0, sem.at[buf]).wait()
            return carry
        lax.fori_loop(0, sum(nch_s[e, step] for e in range(N_EXPERTS)), drain, 0)

    @pl.when(i > 0)
    def _():
        drain_step(i - 1, 1 - slot)

    @pl.when(i == n_ct - 1)
    def _():
        drain_step(i, slot)
        zero_scr[...] = jnp.zeros(zero_scr.shape, zero_scr.dtype)
        fills = 0
        for e in range(N_EXPERTS):
            def fill(c, carry, e=e):
                _chunk_copy(zero_scr, xs_hbm, 0, meta_s[3, e] + c * ROW_ALIGN, sem.at[0]).start()
                return carry
            lax.fori_loop(0, meta_s[4, e], fill, 0)
            fills = fills + meta_s[4, e]

        def drain_fill(c, carry):
            _chunk_copy(zero_scr, xs_hbm, 0, 0, sem.at[0]).wait()
            return carry
        lax.fori_loop(0, fills, drain_fill, 0)

        tm = zero_scr.shape[0]
        n_mt = xs_hbm.shape[0] // tm

        def tile_copy(t):
            return pltpu.make_async_copy(zero_scr, xs_hbm.at[pl.ds(pl.multiple_of(t * tm, tm), tm)], tile_sem)

        def fill_tile(t, carry):
            tile_copy(t).start()
            return carry
        lax.fori_loop(meta_s[5, 0], n_mt, fill_tile, 0)

        def drain_tile(t, carry):
            tile_copy(t).wait()
            return carry
        lax.fori_loop(meta_s[5, 0], n_mt, drain_tile, 0)


def _moe_dispatch(tables, gates_t, h2, tc, tm):
    T, D = h2.shape
    n_ct, local_rows, n_mt = _moe_dims(T, tc, tm)
    upper = (jnp.arange(tc)[:, None] < jnp.arange(tc)[None, :]).astype(BF16)
    return pl.pallas_call(
        _dispatch_kernel,
        grid_spec=pltpu.PrefetchScalarGridSpec(
            num_scalar_prefetch=4,
            grid=(n_ct,),
            in_specs=[pl.BlockSpec((N_EXPERTS, tc), lambda i, *_: (0, i)),
                      pl.BlockSpec((tc, D), lambda i, *_: (i, 0)),
                      pl.BlockSpec((tc, tc), lambda i, *_: (0, 0))],
            out_specs=pl.BlockSpec(memory_space=pl.ANY),
            scratch_shapes=[pltpu.VMEM((2, local_rows, D), BF16), pltpu.VMEM((tm, D), BF16),
                            pltpu.SemaphoreType.DMA((2,)), pltpu.SemaphoreType.DMA(())]),
        out_shape=jax.ShapeDtypeStruct((n_mt * tm, D), BF16),
        compiler_params=pltpu.CompilerParams(
            dimension_semantics=("arbitrary",), vmem_limit_bytes=V7X_VMEM_LIMIT),
        name="moe_dispatch",
    )(*tables, gates_t, h2, upper)


def _experts_kernel(meta_s, xs_ref, w1_ref, w3_ref, w2_ref, ys_ref, acc_scr):
    m = pl.program_id(0)
    f = pl.program_id(1)
    last_f = f == pl.num_programs(1) - 1
    valid = meta_s[1, m]
    tm = xs_ref.shape[0]
    half = tm // 2

    def ffn_rows(rows):
        x = xs_ref[0:rows, :]
        a = _dot(x, w1_ref[0])
        mid = a * _sigmoid(a) * _dot(x, w3_ref[0])
        part = _dot(mid.astype(BF16), w2_ref[0])

        @pl.when(f == 0)
        def _():
            acc_scr[0:rows, :] = part

        @pl.when(f > 0)
        def _():
            acc_scr[0:rows, :] += part

        @pl.when(last_f)
        def _():
            ys_ref[0:rows, :] = acc_scr[0:rows, :].astype(BF16)
            if rows < tm:
                ys_ref[rows:tm, :] = jnp.zeros((tm - rows, ys_ref.shape[1]), ys_ref.dtype)

    @pl.when(valid > half)
    def _():
        ffn_rows(tm)

    @pl.when((valid > 0) & (valid <= half))
    def _():
        ffn_rows(half)

    @pl.when((valid == 0) & last_f)
    def _():
        ys_ref[...] = jnp.zeros(ys_ref.shape, ys_ref.dtype)


def _moe_experts(meta, xs, w1, w3, w2, which, tm, tf):
    rows, D = xs.shape
    F = w1.shape[3]
    nf = F // tf
    w1, w3, w2 = (w.reshape((-1,) + w.shape[2:]) for w in (w1, w3, w2))
    meta = meta.at[0].add(which * N_EXPERTS)

    def f_eff(m, f, meta):
        return jnp.where(meta[1, m] > 0, f, nf - 1)

    return pl.pallas_call(
        _experts_kernel,
        grid_spec=pltpu.PrefetchScalarGridSpec(
            num_scalar_prefetch=1,
            grid=(rows // tm, nf),
            in_specs=[pl.BlockSpec((tm, D), lambda m, f, meta: (meta[2, m], 0)),
                      pl.BlockSpec((1, D, tf), lambda m, f, meta: (meta[0, m], 0, f_eff(m, f, meta))),
                      pl.BlockSpec((1, D, tf), lambda m, f, meta: (meta[0, m], 0, f_eff(m, f, meta))),
                      pl.BlockSpec((1, tf, D), lambda m, f, meta: (meta[0, m], f_eff(m, f, meta), 0))],
            out_specs=pl.BlockSpec((tm, D), lambda m, f, meta: (m, 0)),
            scratch_shapes=[pltpu.VMEM((tm, D), F32)]),
        out_shape=jax.ShapeDtypeStruct((rows, D), BF16),
        compiler_params=pltpu.CompilerParams(
            dimension_semantics=("arbitrary", "arbitrary"), vmem_limit_bytes=V7X_VMEM_LIMIT),
        name="moe_experts",
    )(meta, xs, w1, w3, w2)


def _combine_kernel(goff_s, loc_s, nch_s, gates_ref, x1_ref, upper_ref, ys_hbm, o_ref, yc_scr, sem):
    i = pl.program_id(0)
    tc = gates_ref.shape[1]

    slot = i % 2

    def fetch(step, buf):
        for e in range(N_EXPERTS):
            def issue(c, carry, e=e):
                _chunk_copy(ys_hbm, yc_scr.at[buf], goff_s[e, step] + c * ROW_ALIGN,
                            loc_s[e, step] + c * ROW_ALIGN, sem.at[buf]).start()
                return carry
            lax.fori_loop(0, nch_s[e, step], issue, 0)

    @pl.when(i == 0)
    def _():
        yc_scr[...] = jnp.zeros(yc_scr.shape, yc_scr.dtype)
        fetch(i, slot)

    @pl.when(i + 1 < pl.num_programs(0))
    def _():
        fetch(i + 1, 1 - slot)

    row_a, row_b, gate_a, gate_b = _segment_rows(gates_ref[...], loc_s, i, upper_ref[...])
    stats = jnp.concatenate([row_a.astype(F32), row_b.astype(F32), gate_a, gate_b,
                             jnp.zeros((META_LANES - 4, tc), F32)], axis=0).T
    lane = lax.broadcasted_iota(jnp.int32, (tc, yc_scr.shape[1]), 1).astype(F32)
    sel = jnp.where(lane == stats[:, 0:1], stats[:, 2:3],
                    jnp.where(lane == stats[:, 1:2], stats[:, 3:4], 0.0)).astype(BF16)

    def drain(c, carry):
        _chunk_copy(ys_hbm, yc_scr.at[slot], 0, 0, sem.at[slot]).wait()
        return carry
    lax.fori_loop(0, sum(nch_s[e, i] for e in range(N_EXPERTS)), drain, 0)
    o_ref[...] = x1_ref[...] + _dot(sel, yc_scr[slot])


def _moe_combine(tables, gates_t, x1, ys, tc, tm):
    T, D = x1.shape
    n_ct, local_rows, _ = _moe_dims(T, tc, tm)
    upper = (jnp.arange(tc)[:, None] < jnp.arange(tc)[None, :]).astype(BF16)
    return pl.pallas_call(
        _combine_kernel,
        grid_spec=pltpu.PrefetchScalarGridSpec(
            num_scalar_prefetch=3,
            grid=(n_ct,),
            in_specs=[pl.BlockSpec((N_EXPERTS, tc), lambda i, *_: (0, i)),
                      pl.BlockSpec((tc, D), lambda i, *_: (i, 0)),
                      pl.BlockSpec((tc, tc), lambda i, *_: (0, 0)),
                      pl.BlockSpec(memory_space=pl.ANY)],
            out_specs=pl.BlockSpec((tc, D), lambda i, *_: (i, 0)),
            scratch_shapes=[pltpu.VMEM((2, local_rows, D), BF16), pltpu.SemaphoreType.DMA((2,))]),
        out_shape=jax.ShapeDtypeStruct((T, D), F32),
        compiler_params=pltpu.CompilerParams(
            dimension_semantics=("arbitrary",), vmem_limit_bytes=V7X_VMEM_LIMIT),
        name="moe_combine",
    )(*tables[:3], gates_t, x1, upper, ys)


def _moe(x1, h2, gates_t, w1, w3, w2, which, tc, tm, tf):
    goff, loc, nch, meta = _moe_plan(gates_t, tc, tm)
    xs = _moe_dispatch((goff, loc, nch, meta), gates_t, h2, tc, tm)
    ys = _moe_experts(meta, xs, w1, w3, w2, which, tm, tf)
    return _moe_combine((goff, loc, nch), gates_t, x1, ys, tc, tm)


def _rope_tables(seq):
    pos = jnp.arange(seq, dtype=F32)
    inv = 1.0 / (ROPE_THETA ** (jnp.arange(0, QK_DIM, 2, dtype=F32) / QK_DIM))
    ang = pos[:, None] * inv[None, :]
    cos, sin = jnp.cos(ang), jnp.sin(ang)
    groups = BR // QK_DIM
    cos_t = jnp.tile(jnp.concatenate([cos, cos], axis=-1), (1, groups))
    sin_t = jnp.tile(jnp.concatenate([-sin, sin], axis=-1), (1, groups))
    return cos_t, sin_t


def _block_diag(blocks):
    g, n, m = blocks.shape
    eye = jnp.eye(g, dtype=blocks.dtype)
    return (eye[:, None, :, None] * blocks[:, :, None, :]).reshape(g * n, g * m)


def kernel(x, norm1_g, w_in, w_gate, b_gate, w_branch, w_out, pool_w, pool_scale, attn_qn_g, attn_kn_g,
           lam_q1, lam_k1, lam_q2, lam_k2, attn_subln_g, conv_w, conv_b, conv_ln_g, conv_ln_b,
           sgu_ln_g, sgu_ln_b, sgu_w, sgu_b, norm2_g, ffn_w1, ffn_w3, ffn_w2, moe_router,
           moe_w1, moe_w3, moe_w2):
    B, S, D = x.shape
    depth = w_in.shape[0]
    T = B * S
    tm_mix = min(1024, S)
    tq = min(256, S)
    tm_merge = min(512, T)
    tm_ffn = min(1024, T)
    tm_moe = 512

    cos_t, sin_t = _rope_tables(S)
    bd32 = _block_diag(jnp.ones((BR // QK_DIM, QK_DIM, QK_DIM), BF16))
    row = lambda v: v.reshape(1, -1).astype(F32)

    stacked = {"w_in": w_in, "w_gate": w_gate, "w_branch": w_branch, "w_out": w_out, "ffn_w1": ffn_w1,
               "ffn_w3": ffn_w3, "ffn_w2": ffn_w2}
    moe_hosts = {0: {"moe_w1": moe_w1, "moe_w3": moe_w3}, 1: {"moe_w2": moe_w2}} if depth > 1 else {}
    first_in = w_in[0:1].astype(BF16)
    bf = {}

    for l in range(depth):
        x3 = x.reshape(B, S, D)
        q, k, vt, ya, yc, yd = _mixer_in(
            x3, row(norm1_g[l]), first_in if l == 0 else bf["w_in"], 0 if l == 0 else l, cos_t, sin_t,
            row(jnp.tile(attn_qn_g[l], BR // QK_DIM)), row(jnp.tile(attn_kn_g[l], BR // QK_DIM)), bd32,
            _block_diag(pool_w[l]).astype(BF16), row(pool_scale[l]),
            conv_w[l], row(conv_b[l]), row(conv_ln_g[l]), row(conv_ln_b[l]),
            row(sgu_ln_g[l]), row(sgu_ln_b[l]), sgu_w[l],
            jnp.repeat(sgu_b[l].T, BR // SGU_GROUPS, axis=1), tm_mix)
        lamv = jnp.stack([lam_q1[l], lam_k1[l], lam_q2[l], lam_k2[l]]).astype(F32)
        lam_init = 0.8 - 0.6 * math.exp(-0.3 * l)
        sg_col = jnp.broadcast_to(attn_subln_g[l].astype(F32)[:, None], (V_DIM, tq))
        score_bound = (QK_DIM ** 0.5 * LOG2E) * jnp.max(jnp.abs(attn_qn_g[l])) * jnp.max(jnp.abs(attn_kn_g[l]))
        bounded = (score_bound <= SAFE_SCORE_BOUND).astype(jnp.int32).reshape(1)
        yb, cast = _attention(bounded, lamv, q, k, vt, sg_col, lam_init, tq,
                              side=list(stacked.values()) if l == 0 else ())
        if l == 0:
            bf.update(zip(stacked, cast))

        is_moe = l % 2 == 1
        router_t = moe_router[l // 2].T.astype(F32) if is_moe else None
        flat = lambda y: y.reshape(T, BR)
        hosted = moe_hosts.get(l, {})
        outs, cast = _merge(x.reshape(T, D), flat(ya), flat(yb), flat(yc), flat(yd), row(norm1_g[l]),
                            bf["w_gate"], row(b_gate[l]), bf["w_branch"], bf["w_out"], l, row(norm2_g[l]),
                            router_t, tm_merge, side=list(hosted.values()))
        bf.update(zip(hosted, cast))
        i = l // 2
        if is_moe:
            x1, h2, gates_t = outs
            x = _moe(x1, h2, gates_t, bf["moe_w1"], bf["moe_w3"], bf["moe_w2"], i,
                     tm_merge, tm_moe, moe_w1.shape[3] // 2)
        else:
            x1, h2 = outs
            x = _ffn(x1, h2, bf["ffn_w1"], bf["ffn_w3"], bf["ffn_w2"], i, tm_ffn, FFN_CHUNK)
    return x.reshape(B, S, D)
```

```python
import functools
import math

import jax
import jax.numpy as jnp
from jax import lax
from jax.experimental import pallas as pl
from jax.experimental.pallas import tpu as pltpu

F32 = jnp.float32
BF16 = jnp.bfloat16

N_BRANCH = 4
BR = 256
POOL_WINDOWS = (2, 4, 8, 16)
POOL_GW = 64
HEADS = 4
QK_DIM = 32
V_DIM = 64
ROPE_THETA = 10000.0
CHUNK = 64
CONV_W = 31
SGU_LEN = 128
SGU_GROUPS = 4
N_EXPERTS = 8
EPS = 1e-6
QK_GROUPS = 2 * HEADS
LOG2E = math.log2(math.e)
FFN_CHUNK = 1024
KV_TILES_PER_STEP = 4
SAFE_SCORE_BOUND = 40.0
NEG_INF = -1e30

HALO = 32
V7X_VMEM_LIMIT = 56 * 1024 * 1024


def _rms(x, g):
    return x * lax.rsqrt(jnp.mean(x * x, axis=-1, keepdims=True) + EPS) * g


def _ln(x, g, b):
    mu = jnp.mean(x, axis=-1, keepdims=True)
    xc = x - mu
    return xc * lax.rsqrt(jnp.mean(xc * xc, axis=-1, keepdims=True) + EPS) * g + b


def _sigmoid(x):
    return 0.5 * jnp.tanh(0.5 * x) + 0.5


def _dot(a, b):
    return jnp.dot(a, b, preferred_element_type=F32)


def _mixer_in_kernel(x_ref, xh_ref, g1_ref, win_ref, cos_ref, sin_ref, qg_ref, kg_ref, bd32_ref,
                     poolw_ref, pools_ref, convw_ref, convb_ref, clng_ref, clnb_ref,
                     slng_ref, slnb_ref, sguw_ref, sgub_ref,
                     q_ref, k_ref, vt_ref, ya_ref, yc_ref, yd_ref,
                     a_scr, s2_scr, s4_scr, s8_scr, u_scr, us_scr):
    i = pl.program_id(1)
    tm = x_ref.shape[1]
    g1 = g1_ref[...]
    h = _rms(x_ref[0], g1).astype(BF16)
    proj = _dot(h, win_ref[0])

    hh = _rms(xh_ref[0], g1).astype(BF16)
    halo_ok = (i > 0).astype(F32)
    pa_h = _dot(hh, win_ref[0, :, 0:BR]) * halo_ok
    pc_h = _dot(hh, win_ref[0, :, 4 * BR:6 * BR]) * halo_ok

    a = proj[:, 0:BR]
    a_scr[0:HALO, :] = pa_h
    a_scr[HALO:HALO + tm, :] = a
    n_ext = tm + HALO
    s2_scr[8:n_ext, :] = a_scr[8:n_ext, :] + a_scr[7:n_ext - 1, :]
    s4_scr[16:n_ext, :] = s2_scr[16:n_ext, :] + s2_scr[14:n_ext - 2, :]
    s8_scr[24:n_ext, :] = s4_scr[24:n_ext, :] + s4_scr[20:n_ext - 4, :]
    s16 = s8_scr[HALO:n_ext, :] + s8_scr[HALO - 8:n_ext - 8, :]
    lane = lax.broadcasted_iota(jnp.int32, (tm, BR), 1)
    row = lax.broadcasted_iota(jnp.int32, (tm, BR), 0)
    win_sum = jnp.where(lane < POOL_GW, s2_scr[HALO:n_ext, :],
                        jnp.where(lane < 2 * POOL_GW, s4_scr[HALO:n_ext, :],
                                  jnp.where(lane < 3 * POOL_GW, s8_scr[HALO:n_ext, :], s16)))
    width = jnp.where(lane < POOL_GW, POOL_WINDOWS[0],
                      jnp.where(lane < 2 * POOL_GW, POOL_WINDOWS[1],
                                jnp.where(lane < 3 * POOL_GW, POOL_WINDOWS[2], POOL_WINDOWS[3])))
    cnt = jnp.minimum(width, i * tm + row + 1).astype(F32)
    pooled = win_sum / cnt - a
    ya_ref[0] = (_dot(pooled.astype(BF16), poolw_ref[...]) * pools_ref[...]).astype(BF16)

    lo_half = (lane % QK_DIM) < (QK_DIM // 2)

    def qk_norm_rope(t, g_ref, scale):
        ss = _dot((t * t).astype(BF16), bd32_ref[...])
        tn = t * lax.rsqrt(ss * (1.0 / QK_DIM) + EPS) * g_ref[...]
        rot = jnp.where(lo_half, pltpu.roll(tn, BR - QK_DIM // 2, 1), pltpu.roll(tn, QK_DIM // 2, 1))
        return (tn * cos_ref[...] + rot * sin_ref[...]) * scale

    q_ref[0] = qk_norm_rope(proj[:, BR:2 * BR], qg_ref, QK_DIM ** -0.5 * LOG2E).T.astype(BF16)
    kk = qk_norm_rope(proj[:, 2 * BR:3 * BR], kg_ref, 1.0).astype(BF16)
    for g in range(QK_GROUPS):
        k_ref[0, g] = kk[:, g * QK_DIM:(g + 1) * QK_DIM]
    vt_ref[0] = proj[:, 3 * BR:4 * BR].T.astype(BF16)

    u_scr[0:HALO, :] = pc_h[:, 0:BR] * _sigmoid(pc_h[:, BR:2 * BR])
    u_scr[HALO:HALO + tm, :] = proj[:, 4 * BR:5 * BR] * _sigmoid(proj[:, 5 * BR:6 * BR])
    taps = [convw_ref[j:j + 1, :] for j in range(CONV_W)]
    convb = convb_ref[...]
    clng = clng_ref[...]
    clnb = clnb_ref[...]
    n_sh = us_scr.shape[1]
    for s in range(1, 8):
        us_scr[s - 1] = u_scr[s:s + n_sh, :]
    rc = 64
    base = HALO - (CONV_W - 1)

    def tap_window(c, j):
        s, r0 = (base + j) % 8, c * rc + (base + j) // 8 * 8
        return u_scr[r0:r0 + rc, :] if s == 0 else us_scr[s - 1, r0:r0 + rc, :]

    for c in range(tm // rc):
        acc = convb + taps[0] * tap_window(c, 0)
        for j in range(1, CONV_W):
            acc = acc + taps[j] * tap_window(c, j)
        y = _ln(acc, clng, clnb)
        yc_ref[0, c * rc:(c + 1) * rc, :] = (y * _sigmoid(y)).astype(BF16)

    d_in = proj[:, 6 * BR:8 * BR]
    z = 0.5 * d_in * (1.0 + lax.erf(d_in * (2.0 ** -0.5)))
    vv = _ln(z[:, BR:2 * BR], slng_ref[...], slnb_ref[...]).astype(BF16)
    tr = lax.broadcasted_iota(jnp.int32, (SGU_LEN, SGU_LEN), 0)
    tc = lax.broadcasted_iota(jnp.int32, (SGU_LEN, SGU_LEN), 1)
    wtri = [jnp.where(tr >= tc, sguw_ref[g], 0.0).astype(BF16) for g in range(SGU_GROUPS)]
    lane_g = lax.broadcasted_iota(jnp.int32, (SGU_LEN, BR), 1) // (BR // SGU_GROUPS)
    sgub = sgub_ref[...]
    for n in range(tm // SGU_LEN):
        vn = vv[n * SGU_LEN:(n + 1) * SGU_LEN, :]
        sv = _dot(wtri[0], vn)
        for g in range(1, SGU_GROUPS):
            sv = jnp.where(lane_g == g, _dot(wtri[g], vn), sv)
        yd_ref[0, n * SGU_LEN:(n + 1) * SGU_LEN, :] = (
            z[n * SGU_LEN:(n + 1) * SGU_LEN, 0:BR] * (sv + sgub)).astype(BF16)


def _mixer_in(x3, g1, win, win_l, cos_t, sin_t, qg, kg, bd32, poolw, pools, convw, convb, clng, clnb,
              slng, slnb, sguw, sgub, tm):
    B, S, D = x3.shape
    nt = S // tm
    hb = tm // HALO
    full = lambda *shape: pl.BlockSpec(shape, lambda b, i: (0,) * len(shape))
    in_specs = [
        pl.BlockSpec((1, tm, D), lambda b, i: (b, i, 0)),
        pl.BlockSpec((1, HALO, D), lambda b, i: (b, jnp.maximum(i * hb - 1, 0), 0)),
        full(1, D), pl.BlockSpec((1, D, 8 * BR), lambda b, i: (win_l, 0, 0)),
        pl.BlockSpec((tm, BR), lambda b, i: (i, 0)),
        pl.BlockSpec((tm, BR), lambda b, i: (i, 0)),
        full(1, BR), full(1, BR), full(BR, BR),
        full(BR, BR), full(1, BR), full(CONV_W, BR), full(1, BR), full(1, BR), full(1, BR),
        full(1, BR), full(1, BR), full(SGU_GROUPS, SGU_LEN, SGU_LEN), full(SGU_LEN, BR),
    ]
    tok = lambda: pl.BlockSpec((1, tm, BR), lambda b, i: (b, i, 0))
    split = lambda: pl.BlockSpec((1, QK_GROUPS, tm, QK_DIM), lambda b, i: (b, 0, i, 0))
    flipped = lambda: pl.BlockSpec((1, BR, tm), lambda b, i: (b, 0, i))
    out_specs = [
        flipped(), split(), flipped(),
        tok(), tok(), tok(),
    ]
    out_shape = [
        jax.ShapeDtypeStruct((B, BR, S), BF16),
        jax.ShapeDtypeStruct((B, QK_GROUPS, S, QK_DIM), BF16),
        jax.ShapeDtypeStruct((B, BR, S), BF16),
        jax.ShapeDtypeStruct((B, S, BR), BF16),
        jax.ShapeDtypeStruct((B, S, BR), BF16),
        jax.ShapeDtypeStruct((B, S, BR), BF16),
    ]
    ext = pltpu.VMEM((tm + HALO, BR), F32)
    return pl.pallas_call(
        _mixer_in_kernel,
        grid=(B, nt),
        in_specs=in_specs,
        out_specs=out_specs,
        out_shape=out_shape,
        scratch_shapes=[ext, ext, ext, ext, ext, pltpu.VMEM((7, tm + HALO - 8, BR), F32)],
        compiler_params=pltpu.CompilerParams(
            dimension_semantics=("parallel", "parallel"), vmem_limit_bytes=V7X_VMEM_LIMIT),
        name="mixer_in",
    )(x3, x3, g1, win, cos_t, sin_t, qg, kg, bd32, poolw, pools, convw, convb, clng, clnb,
      slng, slnb, sguw, sgub)


def _attn_kernel(bounded_s, lam_ref, q_ref, k_ref, vt_ref, sg_ref, *rest, lam_init, tq, n_side):
    o_ref, p_scr = rest[n_side], rest[-1]
    _cast_side_blocks(rest[:n_side], rest[n_side + 1:2 * n_side + 1])
    i = pl.program_id(1)
    key_chunk = lax.broadcasted_iota(jnp.int32, (tq, tq), 0) // CHUNK
    qry_chunk = lax.broadcasted_iota(jnp.int32, (tq, tq), 1) // CHUNK
    diag_mask = key_chunk <= qry_chunk

    def scores(g, j, masked):
        k0 = pl.multiple_of(j * tq, tq)
        st = _dot(k_ref[0, g, pl.ds(k0, tq), :], q_ref[0, g * QK_DIM:(g + 1) * QK_DIM, :])
        return jnp.where(diag_mask, st, NEG_INF) if masked else st

    def fold8(x, op):
        return op(x.reshape(tq // 8, 8, tq), axis=0)

    def exact_maxima():
        def max_tile(j, ms, masked):
            return tuple(jnp.maximum(ms[g], fold8(scores(g, j, masked), jnp.max)) for g in range(QK_GROUPS))

        ms = tuple(jnp.full((8, tq), NEG_INF, F32) for _ in range(QK_GROUPS))
        ms = lax.fori_loop(0, i, functools.partial(max_tile, masked=False), ms)
        ms = max_tile(i, ms, True)
        return [jnp.max(x, axis=0, keepdims=True) for x in ms]

    def softmax_pv(shift):
        def acc_tile(j, carry, masked, buf=0):
            k0 = pl.multiple_of(j * tq, tq)
            ls = []
            for g in range(QK_GROUPS):
                st = scores(g, j, masked)
                p = jnp.exp2(st if shift is None else st - shift[g])
                p_scr[buf, g] = p.astype(BF16)
                ls.append(carry[g][0] + fold8(p, jnp.sum))
            out = []
            for g in range(QK_GROUPS):
                hd = g // 2
                vt = vt_ref[0, hd * V_DIM:(hd + 1) * V_DIM, pl.ds(k0, tq)]
                out.append((ls[g], carry[g][1] + _dot(vt, p_scr[buf, g])))
            return tuple(out)

        n_buf = p_scr.shape[0]

        def acc_group(jj, carry):
            for t in range(n_buf):
                carry = acc_tile(n_buf * jj + t, carry, False, t)
            return carry

        init = tuple((jnp.zeros((8, tq), F32), jnp.zeros((V_DIM, tq), F32)) for _ in range(QK_GROUPS))
        carry = lax.fori_loop(0, i // n_buf, acc_group, init)
        carry = lax.fori_loop(i - i % n_buf, i, functools.partial(acc_tile, masked=False), carry)
        carry = acc_tile(i, carry, True)
        return [(jnp.sum(l8, axis=0, keepdims=True), acc) for l8, acc in carry]

    def finish(res):
        lamv = lam_ref[...]
        lam = (jnp.exp(jnp.sum(lamv[0:1] * lamv[1:2], axis=-1, keepdims=True))
               - jnp.exp(jnp.sum(lamv[2:3] * lamv[3:4], axis=-1, keepdims=True)) + lam_init)
        sg = sg_ref[...]
        heads = []
        for hd in range(HEADS):
            o = res[2 * hd][1] / res[2 * hd][0] - lam * (res[2 * hd + 1][1] / res[2 * hd + 1][0])
            o = o * lax.rsqrt(jnp.mean(o * o, axis=0, keepdims=True) + EPS) * sg * (1.0 - lam_init)
            heads.append(o)
        o_ref[0] = jnp.concatenate(heads, axis=0).T.astype(BF16)

    @pl.when(bounded_s[0] == 1)
    def _():
        finish(softmax_pv(None))

    @pl.when(bounded_s[0] != 1)
    def _():
        finish(softmax_pv(exact_maxima()))


def _cast_rows_per_block(rows, steps):
    for rpb in range(ROW_ALIGN, rows + 1, ROW_ALIGN):
        if rows % rpb == 0 and rows // rpb <= steps:
            return rpb
    raise ValueError(f"no aligned row block for {rows} rows in {steps} steps")


def _side_cast_specs(side, steps, step_of):
    side2d = [w.reshape(-1, w.shape[-1]) for w in side]
    specs = []
    for w in side2d:
        rpb = _cast_rows_per_block(w.shape[0], steps)
        last = w.shape[0] // rpb - 1
        specs.append(pl.BlockSpec(
            (rpb, w.shape[1]), lambda *idx, last=last: (jnp.minimum(step_of(*idx), last), 0)))
    return side2d, specs


def _cast_side_blocks(in_refs, out_refs):
    for w_ref, wb_ref in zip(in_refs, out_refs):
        wb_ref[...] = w_ref[...].astype(BF16)


def _attention(bounded, lamv, q, k, vt, sg_col, lam_init, tq, side=()):
    B, _, S = q.shape
    nq = S // tq
    side2d, side_specs = _side_cast_specs(side, B * nq, lambda b, i, *_: b * nq + i)
    outs = pl.pallas_call(
        functools.partial(_attn_kernel, lam_init=lam_init, tq=tq, n_side=len(side2d)),
        grid_spec=pltpu.PrefetchScalarGridSpec(
            num_scalar_prefetch=1,
            grid=(B, nq),
            in_specs=[
                pl.BlockSpec((4, QK_DIM), lambda b, i, *_: (0, 0)),
                pl.BlockSpec((1, BR, tq), lambda b, i, *_: (b, 0, i)),
                pl.BlockSpec((1, QK_GROUPS, S, QK_DIM), lambda b, i, *_: (b, 0, 0, 0)),
                pl.BlockSpec((1, BR, S), lambda b, i, *_: (b, 0, 0)),
                pl.BlockSpec((V_DIM, tq), lambda b, i, *_: (0, 0)),
            ] + side_specs,
            out_specs=[pl.BlockSpec((1, tq, BR), lambda b, i, *_: (b, i, 0))] + side_specs,
            scratch_shapes=[pltpu.VMEM((KV_TILES_PER_STEP, QK_GROUPS, tq, tq), BF16)]),
        out_shape=[jax.ShapeDtypeStruct((B, S, BR), BF16)]
                  + [jax.ShapeDtypeStruct(w.shape, BF16) for w in side2d],
        compiler_params=pltpu.CompilerParams(
            dimension_semantics=("arbitrary", "arbitrary"), vmem_limit_bytes=V7X_VMEM_LIMIT),
        name="diff_attn",
    )(bounded, lamv, q, k, vt, sg_col, *side2d)
    return outs[0], [o.reshape(w.shape) for o, w in zip(outs[1:], side)]


def _merge_kernel(*refs, with_router, n_side):
    n_in = 12 if with_router else 11
    (x_ref, ya_ref, yb_ref, yc_ref, yd_ref, g1_ref, wg_ref, bg_ref, wb_ref, wo_ref, g2_ref) = refs[:11]
    rt_ref = refs[11] if with_router else None
    outs = refs[n_in + n_side:]
    x1_ref, h2_ref = outs[0], outs[1]
    gates_ref = outs[2] if with_router else None
    _cast_side_blocks(refs[n_in:n_in + n_side], outs[len(outs) - n_side:])
    x = x_ref[...]
    d = x.shape[1]
    h = _rms(x, g1_ref[...]).astype(BF16)
    merged = None
    for b, y_ref in enumerate((ya_ref, yb_ref, yc_ref, yd_ref)):
        gate = _sigmoid(_dot(h, wg_ref[0, :, b * d:(b + 1) * d]) + bg_ref[:, b * d:(b + 1) * d])
        term = gate * _dot(y_ref[...], wb_ref[0, b])
        merged = term if merged is None else merged + term
    x1 = x + _dot(merged.astype(BF16), wo_ref[0])
    x1_ref[...] = x1
    h2 = _rms(x1, g2_ref[...])
    h2_ref[...] = h2.astype(BF16)
    if with_router:
        def split(t):
            hi = t.astype(BF16)
            return hi, (t - hi.astype(F32)).astype(BF16)

        def nt_dot(a, b):
            return lax.dot_general(a, b, (((1,), (1,)), ((), ())), preferred_element_type=F32)

        (r_hi, r_lo), (h_hi, h_lo) = split(rt_ref[...]), split(h2)
        logits = nt_dot(r_hi, h_hi) + (nt_dot(r_hi, h_lo) + nt_dot(r_lo, h_hi))
        e_idx = lax.broadcasted_iota(jnp.int32, logits.shape, 0)
        v1 = jnp.max(logits, axis=0, keepdims=True)
        i1 = jnp.min(jnp.where(logits == v1, e_idx, N_EXPERTS), axis=0, keepdims=True)
        rest = jnp.where(e_idx == i1, -jnp.inf, logits)
        v2 = jnp.max(rest, axis=0, keepdims=True)
        i2 = jnp.min(jnp.where(rest == v2, e_idx, N_EXPERTS), axis=0, keepdims=True)
        w2 = 1.0 / (1.0 + jnp.exp(v1 - v2))
        gates_ref[...] = jnp.where(e_idx == i1, 1.0 - w2, jnp.where(e_idx == i2, w2, 0.0))


def _merge(x2, ya, yb, yc, yd, g1, wg, bg, wb, wo, layer, g2, router_t, tm, side=()):
    T, D = x2.shape
    with_router = router_t is not None
    side2d, side_specs = _side_cast_specs(side, T // tm, lambda i: i)
    full = lambda *shape: pl.BlockSpec(shape, lambda i: (0,) * len(shape))
    of_layer = lambda *shape: pl.BlockSpec((1,) + shape, lambda i: (layer,) + (0,) * len(shape))
    tokb = lambda w: pl.BlockSpec((tm, w), lambda i: (i, 0))
    in_specs = [tokb(D), tokb(BR), tokb(BR), tokb(BR), tokb(BR),
                full(1, D), of_layer(D, N_BRANCH * D), full(1, N_BRANCH * D), of_layer(N_BRANCH, BR, D),
                of_layer(D, D), full(1, D)]
    args = [x2, ya, yb, yc, yd, g1, wg, bg, wb, wo, g2]
    out_specs = [tokb(D), tokb(D)]
    out_shape = [jax.ShapeDtypeStruct((T, D), F32), jax.ShapeDtypeStruct((T, D), BF16)]
    if with_router:
        in_specs.append(full(N_EXPERTS, D))
        args.append(router_t)
        out_specs.append(pl.BlockSpec((N_EXPERTS, tm), lambda i: (0, i)))
        out_shape.append(jax.ShapeDtypeStruct((N_EXPERTS, T), F32))
    outs = pl.pallas_call(
        functools.partial(_merge_kernel, with_router=with_router, n_side=len(side2d)),
        grid=(T // tm,),
        in_specs=in_specs + side_specs,
        out_specs=out_specs + side_specs,
        out_shape=out_shape + [jax.ShapeDtypeStruct(w.shape, BF16) for w in side2d],
        compiler_params=pltpu.CompilerParams(
            dimension_semantics=("arbitrary",), vmem_limit_bytes=V7X_VMEM_LIMIT),
        name="merge_router" if with_router else "merge",
    )(*args, *side2d)
    n_main = len(out_shape)
    return outs[:n_main], [o.reshape(w.shape) for o, w in zip(outs[n_main:], side)]


def _ffn_kernel(x1_ref, h2_ref, w1_ref, w3_ref, w2_ref, o_ref, *, tf):
    h2 = h2_ref[...]
    acc = x1_ref[...]
    for c0 in range(0, w1_ref.shape[2], tf):
        c1 = min(c0 + tf, w1_ref.shape[2])
        a = _dot(h2, w1_ref[0, :, c0:c1])
        mid = a * _sigmoid(a) * _dot(h2, w3_ref[0, :, c0:c1])
        acc = acc + _dot(mid.astype(BF16), w2_ref[0, c0:c1, :])
    o_ref[...] = acc


def _ffn(x1, h2, w1, w3, w2, which, tm, tf):
    T, D = x1.shape
    F = w1.shape[2]
    resident = lambda *shape: pl.BlockSpec((1,) + shape, lambda i: (which, 0, 0), pipeline_mode=pl.Buffered(1))
    return pl.pallas_call(
        functools.partial(_ffn_kernel, tf=tf),
        grid=(T // tm,),
        in_specs=[pl.BlockSpec((tm, D), lambda i: (i, 0)),
                  pl.BlockSpec((tm, D), lambda i: (i, 0)),
                  resident(D, F), resident(D, F), resident(F, D)],
        out_specs=pl.BlockSpec((tm, D), lambda i: (i, 0)),
        out_shape=jax.ShapeDtypeStruct((T, D), F32),
        compiler_params=pltpu.CompilerParams(
            dimension_semantics=("parallel",), vmem_limit_bytes=V7X_VMEM_LIMIT),
        name="dense_ffn",
    )(x1, h2, w1, w3, w2)


ROW_ALIGN = 16
META_LANES = 128


def _moe_dims(T, tc, tm):
    n_ct = T // tc
    local_rows = 2 * tc + N_EXPERTS * ROW_ALIGN
    max_rows = 2 * T + n_ct * N_EXPERTS * (ROW_ALIGN - 1)
    n_mt = -(-max_rows // tm) + N_EXPERTS
    assert n_ct <= META_LANES and n_mt <= META_LANES and tm & (tm - 1) == 0
    return n_ct, local_rows, n_mt


def _sublane_excl_scan(x):
    rows, run = [], jnp.zeros_like(x[0:1, :])
    for e in range(x.shape[0]):
        rows.append(run)
        run = run + x[e:e + 1, :]
    return jnp.concatenate(rows, axis=0)


def _plan_kernel(gates_ref, member_ref, tri_ref, goff_ref, loc_ref, nch_ref, meta_ref, *, tm):
    routed = (gates_ref[...] > 0.0).astype(BF16)
    cnt = _dot(routed, member_ref[...]).astype(jnp.int32)
    row_sh, tm_sh = ROW_ALIGN.bit_length() - 1, tm.bit_length() - 1
    pad = ((cnt + (ROW_ALIGN - 1)) >> row_sh) << row_sh
    incl = jnp.dot(pad.astype(F32), tri_ref[...], preferred_element_type=F32,
                   precision=lax.Precision.HIGHEST).astype(jnp.int32)
    tot = jnp.max(incl, axis=1, keepdims=True)
    etot = ((tot + (tm - 1)) >> tm_sh) << tm_sh
    base = _sublane_excl_scan(etot)
    goff_ref[...] = base + incl - pad
    loc_ref[...] = _sublane_excl_scan(pad)
    nch_ref[...] = pad >> row_sh

    lane = lax.broadcasted_iota(jnp.int32, (N_EXPERTS, META_LANES), 1)
    sub = lax.broadcasted_iota(jnp.int32, (N_EXPERTS, META_LANES), 0)
    end = base + etot
    n_used = jnp.max(end, axis=0, keepdims=True) >> tm_sh
    src = jnp.minimum(lane, n_used - 1)
    expert = jnp.minimum(jnp.sum((src * tm >= end).astype(jnp.int32), axis=0, keepdims=True), N_EXPERTS - 1)
    start = lane * tm
    inside = (start >= base) & (start < end)
    valid = jnp.sum(jnp.where(inside, jnp.clip(tot - (start - base), 0, tm), 0), axis=0, keepdims=True)
    fill_start = jnp.sum(jnp.where(sub == lane, base + tot, 0), axis=0, keepdims=True)
    fill_nch = jnp.sum(jnp.where(sub == lane, (etot - tot) >> row_sh, 0), axis=0, keepdims=True)
    meta = jnp.where(sub == 0, expert, jnp.where(sub == 1, valid, jnp.where(sub == 2, src,
                     jnp.where(sub == 3, fill_start, jnp.where(sub == 4, fill_nch,
                     jnp.where(sub == 5, n_used, 0))))))
    meta_ref[...] = meta


def _moe_plan(gates_t, tc, tm):
    E, T = gates_t.shape
    member = (jnp.arange(T)[:, None] // tc == jnp.arange(META_LANES)[None, :]).astype(BF16)
    tri = (jnp.arange(META_LANES)[:, None] <= jnp.arange(META_LANES)[None, :]).astype(F32)
    tbl = jax.ShapeDtypeStruct((N_EXPERTS, META_LANES), jnp.int32)
    return pl.pallas_call(
        functools.partial(_plan_kernel, tm=tm),
        out_shape=[tbl, tbl, tbl, tbl],
        compiler_params=pltpu.CompilerParams(vmem_limit_bytes=V7X_VMEM_LIMIT),
        name="moe_plan",
    )(gates_t, member, tri)


def _segment_rows(gates, loc_s, tile, strict_upper):
    routed = gates > 0.0
    pos = _dot(routed.astype(BF16), strict_upper).astype(jnp.int32)
    sub = lax.broadcasted_iota(jnp.int32, gates.shape, 0)
    loc = jnp.zeros(gates.shape, jnp.int32)
    for e in range(N_EXPERTS):
        loc = jnp.where(sub == e, loc_s[e, tile], loc)
    rows = loc + pos
    row_a = jnp.min(jnp.where(routed, rows, jnp.iinfo(jnp.int32).max), axis=0, keepdims=True)
    row_b = jnp.max(jnp.where(routed, rows, -1), axis=0, keepdims=True)
    gate_a = jnp.sum(jnp.where(routed & (rows == row_a), gates, 0.0), axis=0, keepdims=True)
    gate_b = jnp.sum(jnp.where(routed & (rows == row_b), gates, 0.0), axis=0, keepdims=True)
    return row_a, row_b, gate_a, gate_b


def _segment_copies(src, dst, src_row, dst_row, n_chunks, max_chunks, sem, start):
    done = 0
    piece = 1 << (max_chunks.bit_length() - 1)
    while piece:
        take = n_chunks & piece
        rows = piece * ROW_ALIGN

        @pl.when(take != 0)
        def _(done=done, rows=rows):
            def at(row0):
                row = row0 + done * ROW_ALIGN
                return pl.ds(row if isinstance(row, int) else pl.multiple_of(row, ROW_ALIGN), rows)

            copy = pltpu.make_async_copy(src.at[at(src_row)], dst.at[at(dst_row)], sem)
            copy.start() if start else copy.wait()

        done = done + take
        piece //= 2


def _dispatch_kernel(goff_s, loc_s, nch_s, meta_s, gates_ref, h2_ref, upper_ref, xs_hbm,
                     xc_scr, zero_scr, sem, tile_sem):
    i = pl.program_id(0)
    n_ct = pl.num_programs(0)
    row_a, row_b, _, _ = _segment_rows(gates_ref[...], loc_s, i, upper_ref[...])
    r_iota = lax.broadcasted_iota(jnp.int32, (xc_scr.shape[1], gates_ref.shape[1]), 0)
    onehot = jnp.where(r_iota == row_a, 1.0, jnp.where(r_iota == row_b, 1.0, 0.0)).astype(BF16)
    slot = i % 2
    xc_scr[slot] = _dot(onehot, h2_ref[...]).astype(BF16)

    seg_chunks = gates_ref.shape[1] // ROW_ALIGN + 1

    def segments(step, buf, start):
        for e in range(N_EXPERTS):
            _segment_copies(xc_scr.at[buf], xs_hbm, loc_s[e, step], goff_s[e, step], nch_s[e, step],
                            seg_chunks, sem.at[buf], start)

    segments(i, slot, True)

    @pl.when(i > 0)
    def _():
        segments(i - 1, 1 - slot, False)

    @pl.when(i == n_ct - 1)
    def _():
        segments(i, slot, False)
        zero_scr[...] = jnp.zeros(zero_scr.shape, zero_scr.dtype)
        tm = zero_scr.shape[0]
        for start in (True, False):
            for e in range(N_EXPERTS):
                _segment_copies(zero_scr, xs_hbm, 0, meta_s[3, e], meta_s[4, e], tm // ROW_ALIGN - 1,
                                sem.at[0], start)

        n_mt = xs_hbm.shape[0] // tm

        def tile_copy(t):
            return pltpu.make_async_copy(zero_scr, xs_hbm.at[pl.ds(pl.multiple_of(t * tm, tm), tm)], tile_sem)

        def fill_tile(t, carry):
            tile_copy(t).start()
            return carry
        lax.fori_loop(meta_s[5, 0], n_mt, fill_tile, 0)

        def drain_tile(t, carry):
            tile_copy(t).wait()
            return carry
        lax.fori_loop(meta_s[5, 0], n_mt, drain_tile, 0)


def _moe_dispatch(tables, gates_t, h2, tc, tm):
    T, D = h2.shape
    n_ct, local_rows, n_mt = _moe_dims(T, tc, tm)
    upper = (jnp.arange(tc)[:, None] < jnp.arange(tc)[None, :]).astype(BF16)
    return pl.pallas_call(
        _dispatch_kernel,
        grid_spec=pltpu.PrefetchScalarGridSpec(
            num_scalar_prefetch=4,
            grid=(n_ct,),
            in_specs=[pl.BlockSpec((N_EXPERTS, tc), lambda i, *_: (0, i)),
                      pl.BlockSpec((tc, D), lambda i, *_: (i, 0)),
                      pl.BlockSpec((tc, tc), lambda i, *_: (0, 0))],
            out_specs=pl.BlockSpec(memory_space=pl.ANY),
            scratch_shapes=[pltpu.VMEM((2, local_rows, D), BF16), pltpu.VMEM((tm, D), BF16),
                            pltpu.SemaphoreType.DMA((2,)), pltpu.SemaphoreType.DMA(())]),
        out_shape=jax.ShapeDtypeStruct((n_mt * tm, D), BF16),
        compiler_params=pltpu.CompilerParams(
            dimension_semantics=("arbitrary",), vmem_limit_bytes=V7X_VMEM_LIMIT),
        name="moe_dispatch",
    )(*tables, gates_t, h2, upper)


def _experts_kernel(meta_s, xs_ref, w1_ref, w3_ref, w2_ref, ys_ref, acc_scr):
    m = pl.program_id(0)
    f = pl.program_id(1)
    last_f = f == pl.num_programs(1) - 1
    valid = meta_s[1, m]
    tm = xs_ref.shape[0]
    half = tm // 2

    def ffn_rows(rows):
        x = xs_ref[0:rows, :]
        a = _dot(x, w1_ref[0])
        mid = a * _sigmoid(a) * _dot(x, w3_ref[0])
        part = _dot(mid.astype(BF16), w2_ref[0])

        @pl.when(f == 0)
        def _():
            acc_scr[0:rows, :] = part

        @pl.when(f > 0)
        def _():
            acc_scr[0:rows, :] += part

        @pl.when(last_f)
        def _():
            ys_ref[0:rows, :] = acc_scr[0:rows, :].astype(BF16)
            if rows < tm:
                ys_ref[rows:tm, :] = jnp.zeros((tm - rows, ys_ref.shape[1]), ys_ref.dtype)

    @pl.when(valid > half)
    def _():
        ffn_rows(tm)

    @pl.when((valid > 0) & (valid <= half))
    def _():
        ffn_rows(half)

    @pl.when((valid == 0) & last_f)
    def _():
        ys_ref[...] = jnp.zeros(ys_ref.shape, ys_ref.dtype)


def _moe_experts(meta, xs, w1, w3, w2, which, tm, tf):
    rows, D = xs.shape
    F = w1.shape[3]
    nf = F // tf
    w1, w3, w2 = (w.reshape((-1,) + w.shape[2:]) for w in (w1, w3, w2))
    meta = meta.at[0].add(which * N_EXPERTS)

    def f_eff(m, f, meta):
        return jnp.where(meta[1, m] > 0, f, nf - 1)

    return pl.pallas_call(
        _experts_kernel,
        grid_spec=pltpu.PrefetchScalarGridSpec(
            num_scalar_prefetch=1,
            grid=(rows // tm, nf),
            in_specs=[pl.BlockSpec((tm, D), lambda m, f, meta: (meta[2, m], 0)),
                      pl.BlockSpec((1, D, tf), lambda m, f, meta: (meta[0, m], 0, f_eff(m, f, meta))),
                      pl.BlockSpec((1, D, tf), lambda m, f, meta: (meta[0, m], 0, f_eff(m, f, meta))),
                      pl.BlockSpec((1, tf, D), lambda m, f, meta: (meta[0, m], f_eff(m, f, meta), 0))],
            out_specs=pl.BlockSpec((tm, D), lambda m, f, meta: (m, 0)),
            scratch_shapes=[pltpu.VMEM((tm, D), F32)]),
        out_shape=jax.ShapeDtypeStruct((rows, D), BF16),
        compiler_params=pltpu.CompilerParams(
            dimension_semantics=("arbitrary", "arbitrary"), vmem_limit_bytes=V7X_VMEM_LIMIT),
        name="moe_experts",
    )(meta, xs, w1, w3, w2)


def _combine_kernel(goff_s, loc_s, nch_s, gates_ref, x1_ref, upper_ref, ys_hbm, o_ref, yc_scr, sem):
    i = pl.program_id(0)
    tc = gates_ref.shape[1]

    slot = i % 2

    seg_chunks = tc // ROW_ALIGN + 1

    def segments(step, buf, start):
        for e in range(N_EXPERTS):
            _segment_copies(ys_hbm, yc_scr.at[buf], goff_s[e, step], loc_s[e, step], nch_s[e, step],
                            seg_chunks, sem.at[buf], start)

    @pl.when(i == 0)
    def _():
        yc_scr[...] = jnp.zeros(yc_scr.shape, yc_scr.dtype)
        segments(i, slot, True)

    @pl.when(i + 1 < pl.num_programs(0))
    def _():
        segments(i + 1, 1 - slot, True)

    row_a, row_b, gate_a, gate_b = _segment_rows(gates_ref[...], loc_s, i, upper_ref[...])
    stats = jnp.concatenate([row_a.astype(F32), row_b.astype(F32), gate_a, gate_b,
                             jnp.zeros((META_LANES - 4, tc), F32)], axis=0).T
    lane = lax.broadcasted_iota(jnp.int32, (tc, yc_scr.shape[1]), 1).astype(F32)
    sel = jnp.where(lane == stats[:, 0:1], stats[:, 2:3],
                    jnp.where(lane == stats[:, 1:2], stats[:, 3:4], 0.0)).astype(BF16)

    segments(i, slot, False)
    o_ref[...] = x1_ref[...] + _dot(sel, yc_scr[slot])


def _moe_combine(tables, gates_t, x1, ys, tc, tm):
    T, D = x1.shape
    n_ct, local_rows, _ = _moe_dims(T, tc, tm)
    upper = (jnp.arange(tc)[:, None] < jnp.arange(tc)[None, :]).astype(BF16)
    return pl.pallas_call(
        _combine_kernel,
        grid_spec=pltpu.PrefetchScalarGridSpec(
            num_scalar_prefetch=3,
            grid=(n_ct,),
            in_specs=[pl.BlockSpec((N_EXPERTS, tc), lambda i, *_: (0, i)),
                      pl.BlockSpec((tc, D), lambda i, *_: (i, 0)),
                      pl.BlockSpec((tc, tc), lambda i, *_: (0, 0)),
                      pl.BlockSpec(memory_space=pl.ANY)],
            out_specs=pl.BlockSpec((tc, D), lambda i, *_: (i, 0)),
            scratch_shapes=[pltpu.VMEM((2, local_rows, D), BF16), pltpu.SemaphoreType.DMA((2,))]),
        out_shape=jax.ShapeDtypeStruct((T, D), F32),
        compiler_params=pltpu.CompilerParams(
            dimension_semantics=("arbitrary",), vmem_limit_bytes=V7X_VMEM_LIMIT),
        name="moe_combine",
    )(*tables[:3], gates_t, x1, upper, ys)


def _moe(x1, h2, gates_t, w1, w3, w2, which, tc, tm, tf):
    goff, loc, nch, meta = _moe_plan(gates_t, tc, tm)
    xs = _moe_dispatch((goff, loc, nch, meta), gates_t, h2, tc, tm)
    ys = _moe_experts(meta, xs, w1, w3, w2, which, tm, tf)
    return _moe_combine((goff, loc, nch), gates_t, x1, ys, tc, tm)


def _rope_tables(seq):
    pos = jnp.arange(seq, dtype=F32)
    inv = 1.0 / (ROPE_THETA ** (jnp.arange(0, QK_DIM, 2, dtype=F32) / QK_DIM))
    ang = pos[:, None] * inv[None, :]
    cos, sin = jnp.cos(ang), jnp.sin(ang)
    groups = BR // QK_DIM
    cos_t = jnp.tile(jnp.concatenate([cos, cos], axis=-1), (1, groups))
    sin_t = jnp.tile(jnp.concatenate([-sin, sin], axis=-1), (1, groups))
    return cos_t, sin_t


def _block_diag(blocks):
    g, n, m = blocks.shape
    eye = jnp.eye(g, dtype=blocks.dtype)
    return (eye[:, None, :, None] * blocks[:, :, None, :]).reshape(g * n, g * m)


def kernel(x, norm1_g, w_in, w_gate, b_gate, w_branch, w_out, pool_w, pool_scale, attn_qn_g, attn_kn_g,
           lam_q1, lam_k1, lam_q2, lam_k2, attn_subln_g, conv_w, conv_b, conv_ln_g, conv_ln_b,
           sgu_ln_g, sgu_ln_b, sgu_w, sgu_b, norm2_g, ffn_w1, ffn_w3, ffn_w2, moe_router,
           moe_w1, moe_w3, moe_w2):
    B, S, D = x.shape
    depth = w_in.shape[0]
    T = B * S
    tm_mix = min(1024, S)
    tq = min(256, S)
    tm_merge = min(512, T)
    tm_ffn = min(1024, T)
    tm_moe = 512

    cos_t, sin_t = _rope_tables(S)
    bd32 = _block_diag(jnp.ones((BR // QK_DIM, QK_DIM, QK_DIM), BF16))
    row = lambda v: v.reshape(1, -1).astype(F32)

    stacked = {"w_in": w_in, "w_gate": w_gate, "w_branch": w_branch, "w_out": w_out, "ffn_w1": ffn_w1,
               "ffn_w3": ffn_w3, "ffn_w2": ffn_w2}
    moe_hosts = {0: {"moe_w1": moe_w1, "moe_w3": moe_w3}, 1: {"moe_w2": moe_w2}} if depth > 1 else {}
    first_in = w_in[0:1].astype(BF16)
    bf = {}

    for l in range(depth):
        x3 = x.reshape(B, S, D)
        q, k, vt, ya, yc, yd = _mixer_in(
            x3, row(norm1_g[l]), first_in if l == 0 else bf["w_in"], 0 if l == 0 else l, cos_t, sin_t,
            row(jnp.tile(attn_qn_g[l], BR // QK_DIM)), row(jnp.tile(attn_kn_g[l], BR // QK_DIM)), bd32,
            _block_diag(pool_w[l]).astype(BF16), row(pool_scale[l]),
            conv_w[l], row(conv_b[l]), row(conv_ln_g[l]), row(conv_ln_b[l]),
            row(sgu_ln_g[l]), row(sgu_ln_b[l]), sgu_w[l],
            jnp.repeat(sgu_b[l].T, BR // SGU_GROUPS, axis=1), tm_mix)
        lamv = jnp.stack([lam_q1[l], lam_k1[l], lam_q2[l], lam_k2[l]]).astype(F32)
        lam_init = 0.8 - 0.6 * math.exp(-0.3 * l)
        sg_col = jnp.broadcast_to(attn_subln_g[l].astype(F32)[:, None], (V_DIM, tq))
        score_bound = (QK_DIM ** 0.5 * LOG2E) * jnp.max(jnp.abs(attn_qn_g[l])) * jnp.max(jnp.abs(attn_kn_g[l]))
        bounded = (score_bound <= SAFE_SCORE_BOUND).astype(jnp.int32).reshape(1)
        yb, cast = _attention(bounded, lamv, q, k, vt, sg_col, lam_init, tq,
                              side=list(stacked.values()) if l == 0 else ())
        if l == 0:
            bf.update(zip(stacked, cast))

        is_moe = l % 2 == 1
        router_t = moe_router[l // 2].T.astype(F32) if is_moe else None
        flat = lambda y: y.reshape(T, BR)
        hosted = moe_hosts.get(l, {})
        outs, cast = _merge(x.reshape(T, D), flat(ya), flat(yb), flat(yc), flat(yd), row(norm1_g[l]),
                            bf["w_gate"], row(b_gate[l]), bf["w_branch"], bf["w_out"], l, row(norm2_g[l]),
                            router_t, tm_merge, side=list(hosted.values()))
        bf.update(zip(hosted, cast))
        i = l // 2
        if is_moe:
            x1, h2, gates_t = outs
            x = _moe(x1, h2, gates_t, bf["moe_w1"], bf["moe_w3"], bf["moe_w2"], i,
                     tm_merge, tm_moe, moe_w1.shape[3] // 2)
        else:
            x1, h2 = outs
            x = _ffn(x1, h2, bf["ffn_w1"], bf["ffn_w3"], bf["ffn_w2"], i, tm_ffn, FFN_CHUNK)
    return x.reshape(B, S, D)
```

```python
import functools
import math

import jax
import jax.numpy as jnp
from jax import lax
from jax.experimental import pallas as pl
from jax.experimental.pallas import tpu as pltpu

F32 = jnp.float32
BF16 = jnp.bfloat16

N_BRANCH = 4
BR = 256
POOL_WINDOWS = (2, 4, 8, 16)
POOL_GW = 64
HEADS = 4
QK_DIM = 32
V_DIM = 64
ROPE_THETA = 10000.0
CHUNK = 64
CONV_W = 31
SGU_LEN = 128
SGU_GROUPS = 4
N_EXPERTS = 8
EPS = 1e-6
QK_GROUPS = 2 * HEADS
LOG2E = math.log2(math.e)
FFN_CHUNK = 1024
KV_TILES_PER_STEP = 4
SAFE_SCORE_BOUND = 40.0
NEG_INF = -1e30

HALO = 32
V7X_VMEM_LIMIT = 56 * 1024 * 1024


def _rms(x, g):
    return x * lax.rsqrt(jnp.mean(x * x, axis=-1, keepdims=True) + EPS) * g


def _ln(x, g, b):
    mu = jnp.mean(x, axis=-1, keepdims=True)
    xc = x - mu
    return xc * lax.rsqrt(jnp.mean(xc * xc, axis=-1, keepdims=True) + EPS) * g + b


def _sigmoid(x):
    return 0.5 * jnp.tanh(0.5 * x) + 0.5


def _dot(a, b):
    return jnp.dot(a, b, preferred_element_type=F32)


def _mixer_in_kernel(x_ref, xh_ref, g1_ref, win_ref, cos_ref, sin_ref, qg_ref, kg_ref, bd32_ref,
                     poolw_ref, pools_ref, convw_ref, convb_ref, clng_ref, clnb_ref,
                     slng_ref, slnb_ref, sguw_ref, sgub_ref,
                     q_ref, k_ref, vt_ref, ya_ref, yc_ref, yd_ref,
                     a_scr, s2_scr, s4_scr, s8_scr, u_scr, us_scr):
    i = pl.program_id(1)
    tm = x_ref.shape[1]
    g1 = g1_ref[...]
    h = _rms(x_ref[0], g1).astype(BF16)
    proj = _dot(h, win_ref[0])

    hh = _rms(xh_ref[0], g1).astype(BF16)
    halo_ok = (i > 0).astype(F32)
    pa_h = _dot(hh, win_ref[0, :, 0:BR]) * halo_ok
    pc_h = _dot(hh, win_ref[0, :, 4 * BR:6 * BR]) * halo_ok

    a = proj[:, 0:BR]
    a_scr[0:HALO, :] = pa_h
    a_scr[HALO:HALO + tm, :] = a
    n_ext = tm + HALO
    s2_scr[8:n_ext, :] = a_scr[8:n_ext, :] + a_scr[7:n_ext - 1, :]
    s4_scr[16:n_ext, :] = s2_scr[16:n_ext, :] + s2_scr[14:n_ext - 2, :]
    s8_scr[24:n_ext, :] = s4_scr[24:n_ext, :] + s4_scr[20:n_ext - 4, :]
    s16 = s8_scr[HALO:n_ext, :] + s8_scr[HALO - 8:n_ext - 8, :]
    lane = lax.broadcasted_iota(jnp.int32, (tm, BR), 1)
    row = lax.broadcasted_iota(jnp.int32, (tm, BR), 0)
    win_sum = jnp.where(lane < POOL_GW, s2_scr[HALO:n_ext, :],
                        jnp.where(lane < 2 * POOL_GW, s4_scr[HALO:n_ext, :],
                                  jnp.where(lane < 3 * POOL_GW, s8_scr[HALO:n_ext, :], s16)))
    width = jnp.where(lane < POOL_GW, POOL_WINDOWS[0],
                      jnp.where(lane < 2 * POOL_GW, POOL_WINDOWS[1],
                                jnp.where(lane < 3 * POOL_GW, POOL_WINDOWS[2], POOL_WINDOWS[3])))
    cnt = jnp.minimum(width, i * tm + row + 1).astype(F32)
    pooled = win_sum / cnt - a
    ya_ref[0] = (_dot(pooled.astype(BF16), poolw_ref[...]) * pools_ref[...]).astype(BF16)

    lo_half = (lane % QK_DIM) < (QK_DIM // 2)

    def qk_norm_rope(t, g_ref, scale):
        ss = _dot((t * t).astype(BF16), bd32_ref[...])
        tn = t * lax.rsqrt(ss * (1.0 / QK_DIM) + EPS) * g_ref[...]
        rot = jnp.where(lo_half, pltpu.roll(tn, BR - QK_DIM // 2, 1), pltpu.roll(tn, QK_DIM // 2, 1))
        return (tn * cos_ref[...] + rot * sin_ref[...]) * scale

    q_ref[0] = qk_norm_rope(proj[:, BR:2 * BR], qg_ref, QK_DIM ** -0.5 * LOG2E).T.astype(BF16)
    kk = qk_norm_rope(proj[:, 2 * BR:3 * BR], kg_ref, 1.0).astype(BF16)
    for g in range(QK_GROUPS):
        k_ref[0, g] = kk[:, g * QK_DIM:(g + 1) * QK_DIM]
    vt_ref[0] = proj[:, 3 * BR:4 * BR].T.astype(BF16)

    u_scr[0:HALO, :] = pc_h[:, 0:BR] * _sigmoid(pc_h[:, BR:2 * BR])
    u_scr[HALO:HALO + tm, :] = proj[:, 4 * BR:5 * BR] * _sigmoid(proj[:, 5 * BR:6 * BR])
    taps = [convw_ref[j:j + 1, :] for j in range(CONV_W)]
    convb = convb_ref[...]
    clng = clng_ref[...]
    clnb = clnb_ref[...]
    n_sh = us_scr.shape[1]
    for s in range(1, 8):
        us_scr[s - 1] = u_scr[s:s + n_sh, :]
    rc = 64
    base = HALO - (CONV_W - 1)

    def tap_window(c, j):
        s, r0 = (base + j) % 8, c * rc + (base + j) // 8 * 8
        return u_scr[r0:r0 + rc, :] if s == 0 else us_scr[s - 1, r0:r0 + rc, :]

    for c in range(tm // rc):
        acc = convb + taps[0] * tap_window(c, 0)
        for j in range(1, CONV_W):
            acc = acc + taps[j] * tap_window(c, j)
        y = _ln(acc, clng, clnb)
        yc_ref[0, c * rc:(c + 1) * rc, :] = (y * _sigmoid(y)).astype(BF16)

    d_in = proj[:, 6 * BR:8 * BR]
    z = 0.5 * d_in * (1.0 + lax.erf(d_in * (2.0 ** -0.5)))
    vv = _ln(z[:, BR:2 * BR], slng_ref[...], slnb_ref[...]).astype(BF16)
    tr = lax.broadcasted_iota(jnp.int32, (SGU_LEN, SGU_LEN), 0)
    tc = lax.broadcasted_iota(jnp.int32, (SGU_LEN, SGU_LEN), 1)
    wtri = [jnp.where(tr >= tc, sguw_ref[g], 0.0).astype(BF16) for g in range(SGU_GROUPS)]
    lane_g = lax.broadcasted_iota(jnp.int32, (SGU_LEN, BR), 1) // (BR // SGU_GROUPS)
    sgub = sgub_ref[...]
    for n in range(tm // SGU_LEN):
        vn = vv[n * SGU_LEN:(n + 1) * SGU_LEN, :]
        sv = _dot(wtri[0], vn)
        for g in range(1, SGU_GROUPS):
            sv = jnp.where(lane_g == g, _dot(wtri[g], vn), sv)
        yd_ref[0, n * SGU_LEN:(n + 1) * SGU_LEN, :] = (
            z[n * SGU_LEN:(n + 1) * SGU_LEN, 0:BR] * (sv + sgub)).astype(BF16)


def _mixer_in(x3, g1, win, win_l, cos_t, sin_t, qg, kg, bd32, poolw, pools, convw, convb, clng, clnb,
              slng, slnb, sguw, sgub, tm):
    B, S, D = x3.shape
    nt = S // tm
    hb = tm // HALO
    full = lambda *shape: pl.BlockSpec(shape, lambda b, i: (0,) * len(shape))
    in_specs = [
        pl.BlockSpec((1, tm, D), lambda b, i: (b, i, 0)),
        pl.BlockSpec((1, HALO, D), lambda b, i: (b, jnp.maximum(i * hb - 1, 0), 0)),
        full(1, D), pl.BlockSpec((1, D, 8 * BR), lambda b, i: (win_l, 0, 0)),
        pl.BlockSpec((tm, BR), lambda b, i: (i, 0)),
        pl.BlockSpec((tm, BR), lambda b, i: (i, 0)),
        full(1, BR), full(1, BR), full(BR, BR),
        full(BR, BR), full(1, BR), full(CONV_W, BR), full(1, BR), full(1, BR), full(1, BR),
        full(1, BR), full(1, BR), full(SGU_GROUPS, SGU_LEN, SGU_LEN), full(SGU_LEN, BR),
    ]
    tok = lambda: pl.BlockSpec((1, tm, BR), lambda b, i: (b, i, 0))
    split = lambda: pl.BlockSpec((1, QK_GROUPS, tm, QK_DIM), lambda b, i: (b, 0, i, 0))
    flipped = lambda: pl.BlockSpec((1, BR, tm), lambda b, i: (b, 0, i))
    out_specs = [
        flipped(), split(), flipped(),
        tok(), tok(), tok(),
    ]
    out_shape = [
        jax.ShapeDtypeStruct((B, BR, S), BF16),
        jax.ShapeDtypeStruct((B, QK_GROUPS, S, QK_DIM), BF16),
        jax.ShapeDtypeStruct((B, BR, S), BF16),
        jax.ShapeDtypeStruct((B, S, BR), BF16),
        jax.ShapeDtypeStruct((B, S, BR), BF16),
        jax.ShapeDtypeStruct((B, S, BR), BF16),
    ]
    ext = pltpu.VMEM((tm + HALO, BR), F32)
    return pl.pallas_call(
        _mixer_in_kernel,
        grid=(B, nt),
        in_specs=in_specs,
        out_specs=out_specs,
        out_shape=out_shape,
        scratch_shapes=[ext, ext, ext, ext, ext, pltpu.VMEM((7, tm + HALO - 8, BR), F32)],
        compiler_params=pltpu.CompilerParams(
            dimension_semantics=("parallel", "parallel"), vmem_limit_bytes=V7X_VMEM_LIMIT),
        name="mixer_in",
    )(x3, x3, g1, win, cos_t, sin_t, qg, kg, bd32, poolw, pools, convw, convb, clng, clnb,
      slng, slnb, sguw, sgub)


def _attn_kernel(bounded_s, lam_ref, q_ref, k_ref, vt_ref, sg_ref, *rest, lam_init, tq, n_side):
    o_ref, p_scr = rest[n_side], rest[-1]
    _cast_side_blocks(rest[:n_side], rest[n_side + 1:2 * n_side + 1])
    i = pl.program_id(1)
    key_chunk = lax.broadcasted_iota(jnp.int32, (tq, tq), 0) // CHUNK
    qry_chunk = lax.broadcasted_iota(jnp.int32, (tq, tq), 1) // CHUNK
    diag_mask = key_chunk <= qry_chunk

    def scores(g, j, masked):
        k0 = pl.multiple_of(j * tq, tq)
        st = _dot(k_ref[0, g, pl.ds(k0, tq), :], q_ref[0, g * QK_DIM:(g + 1) * QK_DIM, :])
        return jnp.where(diag_mask, st, NEG_INF) if masked else st

    def fold8(x, op):
        return op(x.reshape(tq // 8, 8, tq), axis=0)

    def exact_maxima():
        def max_tile(j, ms, masked):
            return tuple(jnp.maximum(ms[g], fold8(scores(g, j, masked), jnp.max)) for g in range(QK_GROUPS))

        ms = tuple(jnp.full((8, tq), NEG_INF, F32) for _ in range(QK_GROUPS))
        ms = lax.fori_loop(0, i, functools.partial(max_tile, masked=False), ms)
        ms = max_tile(i, ms, True)
        return [jnp.max(x, axis=0, keepdims=True) for x in ms]

    def softmax_pv(shift):
        def acc_tile(j, carry, masked, buf=0):
            k0 = pl.multiple_of(j * tq, tq)
            ls = []
            for g in range(QK_GROUPS):
                st = scores(g, j, masked)
                p = jnp.exp2(st if shift is None else st - shift[g])
                p_scr[buf, g] = p.astype(BF16)
                ls.append(carry[g][0] + fold8(p, jnp.sum))
            out = []
            for g in range(QK_GROUPS):
                hd = g // 2
                vt = vt_ref[0, hd * V_DIM:(hd + 1) * V_DIM, pl.ds(k0, tq)]
                out.append((ls[g], carry[g][1] + _dot(vt, p_scr[buf, g])))
            return tuple(out)

        n_buf = p_scr.shape[0]

        def acc_group(jj, carry):
            for t in range(n_buf):
                carry = acc_tile(n_buf * jj + t, carry, False, t)
            return carry

        init = tuple((jnp.zeros((8, tq), F32), jnp.zeros((V_DIM, tq), F32)) for _ in range(QK_GROUPS))
        carry = lax.fori_loop(0, i // n_buf, acc_group, init)
        carry = lax.fori_loop(i - i % n_buf, i, functools.partial(acc_tile, masked=False), carry)
        carry = acc_tile(i, carry, True)
        return [(jnp.sum(l8, axis=0, keepdims=True), acc) for l8, acc in carry]

    def finish(res):
        lamv = lam_ref[...]
        lam = (jnp.exp(jnp.sum(lamv[0:1] * lamv[1:2], axis=-1, keepdims=True))
               - jnp.exp(jnp.sum(lamv[2:3] * lamv[3:4], axis=-1, keepdims=True)) + lam_init)
        sg = sg_ref[...]
        heads = []
        for hd in range(HEADS):
            o = res[2 * hd][1] / res[2 * hd][0] - lam * (res[2 * hd + 1][1] / res[2 * hd + 1][0])
            o = o * lax.rsqrt(jnp.mean(o * o, axis=0, keepdims=True) + EPS) * sg * (1.0 - lam_init)
            heads.append(o)
        o_ref[0] = jnp.concatenate(heads, axis=0).T.astype(BF16)

    @pl.when(bounded_s[0] == 1)
    def _():
        finish(softmax_pv(None))

    @pl.when(bounded_s[0] != 1)
    def _():
        finish(softmax_pv(exact_maxima()))


def _cast_rows_per_block(rows, steps):
    for rpb in range(ROW_ALIGN, rows + 1, ROW_ALIGN):
        if rows % rpb == 0 and rows // rpb <= steps:
            return rpb
    raise ValueError(f"no aligned row block for {rows} rows in {steps} steps")


def _side_cast_specs(side, steps, step_of):
    side2d = [w.reshape(-1, w.shape[-1]) for w in side]
    specs = []
    for w in side2d:
        rpb = _cast_rows_per_block(w.shape[0], steps)
        last = w.shape[0] // rpb - 1
        specs.append(pl.BlockSpec(
            (rpb, w.shape[1]), lambda *idx, last=last: (jnp.minimum(step_of(*idx), last), 0)))
    return side2d, specs


def _cast_side_blocks(in_refs, out_refs):
    for w_ref, wb_ref in zip(in_refs, out_refs):
        wb_ref[...] = w_ref[...].astype(BF16)


def _attention(bounded, lamv, q, k, vt, sg_col, lam_init, tq, side=()):
    B, _, S = q.shape
    nq = S // tq
    side2d, side_specs = _side_cast_specs(side, B * nq, lambda b, i, *_: b * nq + i)
    outs = pl.pallas_call(
        functools.partial(_attn_kernel, lam_init=lam_init, tq=tq, n_side=len(side2d)),
        grid_spec=pltpu.PrefetchScalarGridSpec(
            num_scalar_prefetch=1,
            grid=(B, nq),
            in_specs=[
                pl.BlockSpec((4, QK_DIM), lambda b, i, *_: (0, 0)),
                pl.BlockSpec((1, BR, tq), lambda b, i, *_: (b, 0, i)),
                pl.BlockSpec((1, QK_GROUPS, S, QK_DIM), lambda b, i, *_: (b, 0, 0, 0)),
                pl.BlockSpec((1, BR, S), lambda b, i, *_: (b, 0, 0)),
                pl.BlockSpec((V_DIM, tq), lambda b, i, *_: (0, 0)),
            ] + side_specs,
            out_specs=[pl.BlockSpec((1, tq, BR), lambda b, i, *_: (b, i, 0))] + side_specs,
            scratch_shapes=[pltpu.VMEM((KV_TILES_PER_STEP, QK_GROUPS, tq, tq), BF16)]),
        out_shape=[jax.ShapeDtypeStruct((B, S, BR), BF16)]
                  + [jax.ShapeDtypeStruct(w.shape, BF16) for w in side2d],
        compiler_params=pltpu.CompilerParams(
            dimension_semantics=("arbitrary", "arbitrary"), vmem_limit_bytes=V7X_VMEM_LIMIT),
        name="diff_attn",
    )(bounded, lamv, q, k, vt, sg_col, *side2d)
    return outs[0], [o.reshape(w.shape) for o, w in zip(outs[1:], side)]


def _merge_kernel(*refs, with_router, n_side):
    n_in = 12 if with_router else 11
    (x_ref, ya_ref, yb_ref, yc_ref, yd_ref, g1_ref, wg_ref, bg_ref, wb_ref, wo_ref, g2_ref) = refs[:11]
    rt_ref = refs[11] if with_router else None
    outs = refs[n_in + n_side:]
    x1_ref, h2_ref = outs[0], outs[1]
    gates_ref = outs[2] if with_router else None
    _cast_side_blocks(refs[n_in:n_in + n_side], outs[len(outs) - n_side:])
    x = x_ref[...]
    d = x.shape[1]
    h = _rms(x, g1_ref[...]).astype(BF16)
    merged = None
    for b, y_ref in enumerate((ya_ref, yb_ref, yc_ref, yd_ref)):
        gate = _sigmoid(_dot(h, wg_ref[0, :, b * d:(b + 1) * d]) + bg_ref[:, b * d:(b + 1) * d])
        term = gate * _dot(y_ref[...], wb_ref[0, b])
        merged = term if merged is None else merged + term
    x1 = x + _dot(merged.astype(BF16), wo_ref[0])
    x1_ref[...] = x1
    h2 = _rms(x1, g2_ref[...])
    h2_ref[...] = h2.astype(BF16)
    if with_router:
        def split(t):
            hi = t.astype(BF16)
            return hi, (t - hi.astype(F32)).astype(BF16)

        def nt_dot(a, b):
            return lax.dot_general(a, b, (((1,), (1,)), ((), ())), preferred_element_type=F32)

        (r_hi, r_lo), (h_hi, h_lo) = split(rt_ref[...]), split(h2)
        logits = nt_dot(r_hi, h_hi) + (nt_dot(r_hi, h_lo) + nt_dot(r_lo, h_hi))
        e_idx = lax.broadcasted_iota(jnp.int32, logits.shape, 0)
        v1 = jnp.max(logits, axis=0, keepdims=True)
        i1 = jnp.min(jnp.where(logits == v1, e_idx, N_EXPERTS), axis=0, keepdims=True)
        rest = jnp.where(e_idx == i1, -jnp.inf, logits)
        v2 = jnp.max(rest, axis=0, keepdims=True)
        i2 = jnp.min(jnp.where(rest == v2, e_idx, N_EXPERTS), axis=0, keepdims=True)
        w2 = 1.0 / (1.0 + jnp.exp(v1 - v2))
        gates_ref[...] = jnp.where(e_idx == i1, 1.0 - w2, jnp.where(e_idx == i2, w2, 0.0))


def _merge(x2, ya, yb, yc, yd, g1, wg, bg, wb, wo, layer, g2, router_t, tm, side=()):
    T, D = x2.shape
    with_router = router_t is not None
    side2d, side_specs = _side_cast_specs(side, T // tm, lambda i: i)
    full = lambda *shape: pl.BlockSpec(shape, lambda i: (0,) * len(shape))
    of_layer = lambda *shape: pl.BlockSpec((1,) + shape, lambda i: (layer,) + (0,) * len(shape))
    tokb = lambda w: pl.BlockSpec((tm, w), lambda i: (i, 0))
    in_specs = [tokb(D), tokb(BR), tokb(BR), tokb(BR), tokb(BR),
                full(1, D), of_layer(D, N_BRANCH * D), full(1, N_BRANCH * D), of_layer(N_BRANCH, BR, D),
                of_layer(D, D), full(1, D)]
    args = [x2, ya, yb, yc, yd, g1, wg, bg, wb, wo, g2]
    out_specs = [tokb(D), tokb(D)]
    out_shape = [jax.ShapeDtypeStruct((T, D), F32), jax.ShapeDtypeStruct((T, D), BF16)]
    if with_router:
        in_specs.append(full(N_EXPERTS, D))
        args.append(router_t)
        out_specs.append(pl.BlockSpec((N_EXPERTS, tm), lambda i: (0, i)))
        out_shape.append(jax.ShapeDtypeStruct((N_EXPERTS, T), F32))
    outs = pl.pallas_call(
        functools.partial(_merge_kernel, with_router=with_router, n_side=len(side2d)),
        grid=(T // tm,),
        in_specs=in_specs + side_specs,
        out_specs=out_specs + side_specs,
        out_shape=out_shape + [jax.ShapeDtypeStruct(w.shape, BF16) for w in side2d],
        compiler_params=pltpu.CompilerParams(
            dimension_semantics=("arbitrary",), vmem_limit_bytes=V7X_VMEM_LIMIT),
        name="merge_router" if with_router else "merge",
    )(*args, *side2d)
    n_main = len(out_shape)
    return outs[:n_main], [o.reshape(w.shape) for o, w in zip(outs[n_main:], side)]


def _ffn_kernel(x1_ref, h2_ref, w1_ref, w3_ref, w2_ref, o_ref, *, tf):
    h2 = h2_ref[...]
    acc = x1_ref[...]
    for c0 in range(0, w1_ref.shape[2], tf):
        c1 = min(c0 + tf, w1_ref.shape[2])
        a = _dot(h2, w1_ref[0, :, c0:c1])
        mid = a * _sigmoid(a) * _dot(h2, w3_ref[0, :, c0:c1])
        acc = acc + _dot(mid.astype(BF16), w2_ref[0, c0:c1, :])
    o_ref[...] = acc


def _ffn(x1, h2, w1, w3, w2, which, tm, tf):
    T, D = x1.shape
    F = w1.shape[2]
    resident = lambda *shape: pl.BlockSpec((1,) + shape, lambda i: (which, 0, 0), pipeline_mode=pl.Buffered(1))
    return pl.pallas_call(
        functools.partial(_ffn_kernel, tf=tf),
        grid=(T // tm,),
        in_specs=[pl.BlockSpec((tm, D), lambda i: (i, 0)),
                  pl.BlockSpec((tm, D), lambda i: (i, 0)),
                  resident(D, F), resident(D, F), resident(F, D)],
        out_specs=pl.BlockSpec((tm, D), lambda i: (i, 0)),
        out_shape=jax.ShapeDtypeStruct((T, D), F32),
        compiler_params=pltpu.CompilerParams(
            dimension_semantics=("parallel",), vmem_limit_bytes=V7X_VMEM_LIMIT),
        name="dense_ffn",
    )(x1, h2, w1, w3, w2)


ROW_ALIGN = 16
META_LANES = 128


def _moe_dims(T, tc, tm):
    n_ct = T // tc
    local_rows = 2 * tc + N_EXPERTS * ROW_ALIGN
    max_rows = 2 * T + n_ct * N_EXPERTS * (ROW_ALIGN - 1)
    n_mt = -(-max_rows // tm) + N_EXPERTS
    assert n_ct <= META_LANES and n_mt <= META_LANES and tm & (tm - 1) == 0
    return n_ct, local_rows, n_mt


def _sublane_excl_scan(x):
    rows, run = [], jnp.zeros_like(x[0:1, :])
    for e in range(x.shape[0]):
        rows.append(run)
        run = run + x[e:e + 1, :]
    return jnp.concatenate(rows, axis=0)


def _plan_kernel(gates_ref, member_ref, tri_ref, goff_ref, loc_ref, nch_ref, meta_ref, *, tm):
    routed = (gates_ref[...] > 0.0).astype(BF16)
    cnt = _dot(routed, member_ref[...]).astype(jnp.int32)
    row_sh, tm_sh = ROW_ALIGN.bit_length() - 1, tm.bit_length() - 1
    pad = ((cnt + (ROW_ALIGN - 1)) >> row_sh) << row_sh
    incl = jnp.dot(pad.astype(F32), tri_ref[...], preferred_element_type=F32,
                   precision=lax.Precision.HIGHEST).astype(jnp.int32)
    tot = jnp.max(incl, axis=1, keepdims=True)
    etot = ((tot + (tm - 1)) >> tm_sh) << tm_sh
    base = _sublane_excl_scan(etot)
    goff_ref[...] = base + incl - pad
    loc_ref[...] = _sublane_excl_scan(pad)
    nch_ref[...] = pad >> row_sh

    lane = lax.broadcasted_iota(jnp.int32, (N_EXPERTS, META_LANES), 1)
    sub = lax.broadcasted_iota(jnp.int32, (N_EXPERTS, META_LANES), 0)
    end = base + etot
    n_used = jnp.max(end, axis=0, keepdims=True) >> tm_sh
    src = jnp.minimum(lane, n_used - 1)
    expert = jnp.minimum(jnp.sum((src * tm >= end).astype(jnp.int32), axis=0, keepdims=True), N_EXPERTS - 1)
    start = lane * tm
    inside = (start >= base) & (start < end)
    valid = jnp.sum(jnp.where(inside, jnp.clip(tot - (start - base), 0, tm), 0), axis=0, keepdims=True)
    fill_start = jnp.sum(jnp.where(sub == lane, base + tot, 0), axis=0, keepdims=True)
    fill_nch = jnp.sum(jnp.where(sub == lane, (etot - tot) >> row_sh, 0), axis=0, keepdims=True)
    meta = jnp.where(sub == 0, expert, jnp.where(sub == 1, valid, jnp.where(sub == 2, src,
                     jnp.where(sub == 3, fill_start, jnp.where(sub == 4, fill_nch,
                     jnp.where(sub == 5, n_used, 0))))))
    meta_ref[...] = meta


def _moe_plan(gates_t, tc, tm):
    E, T = gates_t.shape
    member = (jnp.arange(T)[:, None] // tc == jnp.arange(META_LANES)[None, :]).astype(BF16)
    tri = (jnp.arange(META_LANES)[:, None] <= jnp.arange(META_LANES)[None, :]).astype(F32)
    tbl = jax.ShapeDtypeStruct((N_EXPERTS, META_LANES), jnp.int32)
    return pl.pallas_call(
        functools.partial(_plan_kernel, tm=tm),
        out_shape=[tbl, tbl, tbl, tbl],
        compiler_params=pltpu.CompilerParams(vmem_limit_bytes=V7X_VMEM_LIMIT),
        name="moe_plan",
    )(gates_t, member, tri)


def _segment_rows(gates, loc_s, tile, strict_upper):
    routed = gates > 0.0
    pos = _dot(routed.astype(BF16), strict_upper).astype(jnp.int32)
    sub = lax.broadcasted_iota(jnp.int32, gates.shape, 0)
    loc = jnp.zeros(gates.shape, jnp.int32)
    for e in range(N_EXPERTS):
        loc = jnp.where(sub == e, loc_s[e, tile], loc)
    rows = loc + pos
    row_a = jnp.min(jnp.where(routed, rows, jnp.iinfo(jnp.int32).max), axis=0, keepdims=True)
    row_b = jnp.max(jnp.where(routed, rows, -1), axis=0, keepdims=True)
    gate_a = jnp.sum(jnp.where(routed & (rows == row_a), gates, 0.0), axis=0, keepdims=True)
    gate_b = jnp.sum(jnp.where(routed & (rows == row_b), gates, 0.0), axis=0, keepdims=True)
    return row_a, row_b, gate_a, gate_b


def _segment_copies(src, dst, src_row, dst_row, n_chunks, max_chunks, sem, start):
    done = 0
    piece = 1 << (max_chunks.bit_length() - 1)
    while piece:
        take = n_chunks & piece
        rows = piece * ROW_ALIGN

        @pl.when(take != 0)
        def _(done=done, rows=rows):
            def at(row0):
                row = row0 + done * ROW_ALIGN
                return pl.ds(row if isinstance(row, int) else pl.multiple_of(row, ROW_ALIGN), rows)

            copy = pltpu.make_async_copy(src.at[at(src_row)], dst.at[at(dst_row)], sem)
            copy.start() if start else copy.wait()

        done = done + take
        piece //= 2


def _dispatch_kernel(goff_s, loc_s, nch_s, meta_s, gates_ref, h2_ref, upper_ref, xs_hbm,
                     xc_scr, zero_scr, sem, tile_sem):
    i = pl.program_id(0)
    n_ct = pl.num_programs(0)
    row_a, row_b, _, _ = _segment_rows(gates_ref[...], loc_s, i, upper_ref[...])
    r_iota = lax.broadcasted_iota(jnp.int32, (xc_scr.shape[1], gates_ref.shape[1]), 0)
    onehot = jnp.where(r_iota == row_a, 1.0, jnp.where(r_iota == row_b, 1.0, 0.0)).astype(BF16)
    slot = i % 2
    xc_scr[slot] = _dot(onehot, h2_ref[...]).astype(BF16)

    seg_chunks = gates_ref.shape[1] // ROW_ALIGN + 1

    def segments(step, buf, start):
        for e in range(N_EXPERTS):
            _segment_copies(xc_scr.at[buf], xs_hbm, loc_s[e, step], goff_s[e, step], nch_s[e, step],
                            seg_chunks, sem.at[buf], start)

    segments(i, slot, True)

    @pl.when(i > 0)
    def _():
        segments(i - 1, 1 - slot, False)

    @pl.when(i == n_ct - 1)
    def _():
        segments(i, slot, False)
        zero_scr[...] = jnp.zeros(zero_scr.shape, zero_scr.dtype)
        tm = zero_scr.shape[0]
        for start in (True, False):
            for e in range(N_EXPERTS):
                _segment_copies(zero_scr, xs_hbm, 0, meta_s[3, e], meta_s[4, e], tm // ROW_ALIGN - 1,
                                sem.at[0], start)

        n_mt = xs_hbm.shape[0] // tm

        def tile_copy(t):
            return pltpu.make_async_copy(zero_scr, xs_hbm.at[pl.ds(pl.multiple_of(t * tm, tm), tm)], tile_sem)

        def fill_tile(t, carry):
            tile_copy(t).start()
            return carry
        lax.fori_loop(meta_s[5, 0], n_mt, fill_tile, 0)

        def drain_tile(t, carry):
            tile_copy(t).wait()
            return carry
        lax.fori_loop(meta_s[5, 0], n_mt, drain_tile, 0)


def _moe_dispatch(tables, gates_t, h2, tc, tm):
    T, D = h2.shape
    n_ct, local_rows, n_mt = _moe_dims(T, tc, tm)
    upper = (jnp.arange(tc)[:, None] < jnp.arange(tc)[None, :]).astype(BF16)
    return pl.pallas_call(
        _dispatch_kernel,
        grid_spec=pltpu.PrefetchScalarGridSpec(
            num_scalar_prefetch=4,
            grid=(n_ct,),
            in_specs=[pl.BlockSpec((N_EXPERTS, tc), lambda i, *_: (0, i)),
                      pl.BlockSpec((tc, D), lambda i, *_: (i, 0)),
                      pl.BlockSpec((tc, tc), lambda i, *_: (0, 0))],
            out_specs=pl.BlockSpec(memory_space=pl.ANY),
            scratch_shapes=[pltpu.VMEM((2, local_rows, D), BF16), pltpu.VMEM((tm, D), BF16),
                            pltpu.SemaphoreType.DMA((2,)), pltpu.SemaphoreType.DMA(())]),
        out_shape=jax.ShapeDtypeStruct((n_mt * tm, D), BF16),
        compiler_params=pltpu.CompilerParams(
            dimension_semantics=("arbitrary",), vmem_limit_bytes=V7X_VMEM_LIMIT),
        name="moe_dispatch",
    )(*tables, gates_t, h2, upper)


def _experts_kernel(meta_s, xs_ref, w1_ref, w3_ref, w2_ref, ys_ref, acc_scr, *, nf):
    m = pl.program_id(0)
    f = pl.program_id(1)
    last_f = f == nf - 1
    valid = meta_s[1, m]
    tm = xs_ref.shape[0]
    half = tm // 2

    def ffn_rows(rows):
        x = xs_ref[0:rows, :]
        a = _dot(x, w1_ref[0])
        mid = a * _sigmoid(a) * _dot(x, w3_ref[0])
        part = _dot(mid.astype(BF16), w2_ref[0])

        if nf > 1:
            @pl.when(f == 0)
            def _():
                acc_scr[0:rows, :] = part

        if nf > 2:
            @pl.when((f > 0) & (f < nf - 1))
            def _():
                acc_scr[0:rows, :] += part

        @pl.when(last_f)
        def _():
            total = part if nf == 1 else acc_scr[0:rows, :] + part
            ys_ref[0:rows, :] = total.astype(BF16)
            if rows < tm:
                ys_ref[rows:tm, :] = jnp.zeros((tm - rows, ys_ref.shape[1]), ys_ref.dtype)

    @pl.when(valid > half)
    def _():
        ffn_rows(tm)

    @pl.when((valid > 0) & (valid <= half))
    def _():
        ffn_rows(half)

    @pl.when((valid == 0) & last_f)
    def _():
        ys_ref[...] = jnp.zeros(ys_ref.shape, ys_ref.dtype)


def _moe_experts(meta, xs, w1, w3, w2, which, tm, tf):
    rows, D = xs.shape
    F = w1.shape[3]
    nf = F // tf
    w1, w3, w2 = (w.reshape((-1,) + w.shape[2:]) for w in (w1, w3, w2))
    meta = meta.at[0].add(which * N_EXPERTS)

    def f_eff(m, f, meta):
        return jnp.where(meta[1, m] > 0, f, nf - 1)

    return pl.pallas_call(
        functools.partial(_experts_kernel, nf=nf),
        grid_spec=pltpu.PrefetchScalarGridSpec(
            num_scalar_prefetch=1,
            grid=(rows // tm, nf),
            in_specs=[pl.BlockSpec((tm, D), lambda m, f, meta: (meta[2, m], 0)),
                      pl.BlockSpec((1, D, tf), lambda m, f, meta: (meta[0, m], 0, f_eff(m, f, meta))),
                      pl.BlockSpec((1, D, tf), lambda m, f, meta: (meta[0, m], 0, f_eff(m, f, meta))),
                      pl.BlockSpec((1, tf, D), lambda m, f, meta: (meta[0, m], f_eff(m, f, meta), 0))],
            out_specs=pl.BlockSpec((tm, D), lambda m, f, meta: (m, 0)),
            scratch_shapes=[pltpu.VMEM((tm, D), F32)]),
        out_shape=jax.ShapeDtypeStruct((rows, D), BF16),
        compiler_params=pltpu.CompilerParams(
            dimension_semantics=("arbitrary", "arbitrary"), vmem_limit_bytes=V7X_VMEM_LIMIT),
        name="moe_experts",
    )(meta, xs, w1, w3, w2)


def _combine_kernel(goff_s, loc_s, nch_s, gates_ref, x1_ref, upper_ref, ys_hbm, o_ref, yc_scr, sem):
    i = pl.program_id(0)
    tc = gates_ref.shape[1]

    slot = i % 2

    seg_chunks = tc // ROW_ALIGN + 1

    def segments(step, buf, start):
        for e in range(N_EXPERTS):
            _segment_copies(ys_hbm, yc_scr.at[buf], goff_s[e, step], loc_s[e, step], nch_s[e, step],
                            seg_chunks, sem.at[buf], start)

    @pl.when(i == 0)
    def _():
        yc_scr[...] = jnp.zeros(yc_scr.shape, yc_scr.dtype)
        segments(i, slot, True)

    @pl.when(i + 1 < pl.num_programs(0))
    def _():
        segments(i + 1, 1 - slot, True)

    row_a, row_b, gate_a, gate_b = _segment_rows(gates_ref[...], loc_s, i, upper_ref[...])
    stats = jnp.concatenate([row_a.astype(F32), row_b.astype(F32), gate_a, gate_b,
                             jnp.zeros((META_LANES - 4, tc), F32)], axis=0).T
    lane = lax.broadcasted_iota(jnp.int32, (tc, yc_scr.shape[1]), 1).astype(F32)
    sel = jnp.where(lane == stats[:, 0:1], stats[:, 2:3],
                    jnp.where(lane == stats[:, 1:2], stats[:, 3:4], 0.0)).astype(BF16)

    segments(i, slot, False)
    o_ref[...] = x1_ref[...] + _dot(sel, yc_scr[slot])


def _moe_combine(tables, gates_t, x1, ys, tc, tm):
    T, D = x1.shape
    n_ct, local_rows, _ = _moe_dims(T, tc, tm)
    upper = (jnp.arange(tc)[:, None] < jnp.arange(tc)[None, :]).astype(BF16)
    return pl.pallas_call(
        _combine_kernel,
        grid_spec=pltpu.PrefetchScalarGridSpec(
            num_scalar_prefetch=3,
            grid=(n_ct,),
            in_specs=[pl.BlockSpec((N_EXPERTS, tc), lambda i, *_: (0, i)),
                      pl.BlockSpec((tc, D), lambda i, *_: (i, 0)),
                      pl.BlockSpec((tc, tc), lambda i, *_: (0, 0)),
                      pl.BlockSpec(memory_space=pl.ANY)],
            out_specs=pl.BlockSpec((tc, D), lambda i, *_: (i, 0)),
            scratch_shapes=[pltpu.VMEM((2, local_rows, D), BF16), pltpu.SemaphoreType.DMA((2,))]),
        out_shape=jax.ShapeDtypeStruct((T, D), F32),
        compiler_params=pltpu.CompilerParams(
            dimension_semantics=("arbitrary",), vmem_limit_bytes=V7X_VMEM_LIMIT),
        name="moe_combine",
    )(*tables[:3], gates_t, x1, upper, ys)


def _moe(x1, h2, gates_t, w1, w3, w2, which, tc, tm, tf):
    goff, loc, nch, meta = _moe_plan(gates_t, tc, tm)
    xs = _moe_dispatch((goff, loc, nch, meta), gates_t, h2, tc, tm)
    ys = _moe_experts(meta, xs, w1, w3, w2, which, tm, tf)
    return _moe_combine((goff, loc, nch), gates_t, x1, ys, tc, tm)


def _rope_tables(seq):
    pos = jnp.arange(seq, dtype=F32)
    inv = 1.0 / (ROPE_THETA ** (jnp.arange(0, QK_DIM, 2, dtype=F32) / QK_DIM))
    ang = pos[:, None] * inv[None, :]
    cos, sin = jnp.cos(ang), jnp.sin(ang)
    groups = BR // QK_DIM
    cos_t = jnp.tile(jnp.concatenate([cos, cos], axis=-1), (1, groups))
    sin_t = jnp.tile(jnp.concatenate([-sin, sin], axis=-1), (1, groups))
    return cos_t, sin_t


def _block_diag(blocks):
    g, n, m = blocks.shape
    eye = jnp.eye(g, dtype=blocks.dtype)
    return (eye[:, None, :, None] * blocks[:, :, None, :]).reshape(g * n, g * m)


def kernel(x, norm1_g, w_in, w_gate, b_gate, w_branch, w_out, pool_w, pool_scale, attn_qn_g, attn_kn_g,
           lam_q1, lam_k1, lam_q2, lam_k2, attn_subln_g, conv_w, conv_b, conv_ln_g, conv_ln_b,
           sgu_ln_g, sgu_ln_b, sgu_w, sgu_b, norm2_g, ffn_w1, ffn_w3, ffn_w2, moe_router,
           moe_w1, moe_w3, moe_w2):
    B, S, D = x.shape
    depth = w_in.shape[0]
    T = B * S
    tm_mix = min(1024, S)
    tq = min(256, S)
    tm_merge = min(512, T)
    tm_ffn = min(1024, T)
    tm_moe = 512

    cos_t, sin_t = _rope_tables(S)
    bd32 = _block_diag(jnp.ones((BR // QK_DIM, QK_DIM, QK_DIM), BF16))
    row = lambda v: v.reshape(1, -1).astype(F32)

    stacked = {"w_in": w_in, "w_gate": w_gate, "w_branch": w_branch, "w_out": w_out, "ffn_w1": ffn_w1,
               "ffn_w3": ffn_w3, "ffn_w2": ffn_w2}
    moe_hosts = {0: {"moe_w1": moe_w1, "moe_w3": moe_w3}, 1: {"moe_w2": moe_w2}} if depth > 1 else {}
    first_in = w_in[0:1].astype(BF16)
    bf = {}

    for l in range(depth):
        x3 = x.reshape(B, S, D)
        q, k, vt, ya, yc, yd = _mixer_in(
            x3, row(norm1_g[l]), first_in if l == 0 else bf["w_in"], 0 if l == 0 else l, cos_t, sin_t,
            row(jnp.tile(attn_qn_g[l], BR // QK_DIM)), row(jnp.tile(attn_kn_g[l], BR // QK_DIM)), bd32,
            _block_diag(pool_w[l]).astype(BF16), row(pool_scale[l]),
            conv_w[l], row(conv_b[l]), row(conv_ln_g[l]), row(conv_ln_b[l]),
            row(sgu_ln_g[l]), row(sgu_ln_b[l]), sgu_w[l],
            jnp.repeat(sgu_b[l].T, BR // SGU_GROUPS, axis=1), tm_mix)
        lamv = jnp.stack([lam_q1[l], lam_k1[l], lam_q2[l], lam_k2[l]]).astype(F32)
        lam_init = 0.8 - 0.6 * math.exp(-0.3 * l)
        sg_col = jnp.broadcast_to(attn_subln_g[l].astype(F32)[:, None], (V_DIM, tq))
        score_bound = (QK_DIM ** 0.5 * LOG2E) * jnp.max(jnp.abs(attn_qn_g[l])) * jnp.max(jnp.abs(attn_kn_g[l]))
        bounded = (score_bound <= SAFE_SCORE_BOUND).astype(jnp.int32).reshape(1)
        yb, cast = _attention(bounded, lamv, q, k, vt, sg_col, lam_init, tq,
                              side=list(stacked.values()) if l == 0 else ())
        if l == 0:
            bf.update(zip(stacked, cast))

        is_moe = l % 2 == 1
        router_t = moe_router[l // 2].T.astype(F32) if is_moe else None
        flat = lambda y: y.reshape(T, BR)
        hosted = moe_hosts.get(l, {})
        outs, cast = _merge(x.reshape(T, D), flat(ya), flat(yb), flat(yc), flat(yd), row(norm1_g[l]),
                            bf["w_gate"], row(b_gate[l]), bf["w_branch"], bf["w_out"], l, row(norm2_g[l]),
                            router_t, tm_merge, side=list(hosted.values()))
        bf.update(zip(hosted, cast))
        i = l // 2
        if is_moe:
            x1, h2, gates_t = outs
            x = _moe(x1, h2, gates_t, bf["moe_w1"], bf["moe_w3"], bf["moe_w2"], i,
                     tm_merge, tm_moe, moe_w1.shape[3] // 2)
        else:
            x1, h2 = outs
            x = _ffn(x1, h2, bf["ffn_w1"], bf["ffn_w3"], bf["ffn_w2"], i, tm_ffn, FFN_CHUNK)
    return x.reshape(B, S, D)
```

```python
import functools
import math

import jax
import jax.numpy as jnp
from jax import lax
from jax.experimental import pallas as pl
from jax.experimental.pallas import tpu as pltpu

F32 = jnp.float32
BF16 = jnp.bfloat16

N_BRANCH = 4
BR = 256
POOL_WINDOWS = (2, 4, 8, 16)
POOL_GW = 64
HEADS = 4
QK_DIM = 32
V_DIM = 64
ROPE_THETA = 10000.0
CHUNK = 64
CONV_W = 31
SGU_LEN = 128
SGU_GROUPS = 4
N_EXPERTS = 8
EPS = 1e-6
QK_GROUPS = 2 * HEADS
LOG2E = math.log2(math.e)
FFN_CHUNK = 1024
KV_TILES_PER_STEP = 4
SAFE_SCORE_BOUND = 40.0
NEG_INF = -1e30

HALO = 32
V7X_VMEM_LIMIT = 56 * 1024 * 1024


def _rms(x, g):
    return x * lax.rsqrt(jnp.mean(x * x, axis=-1, keepdims=True) + EPS) * g


def _ln(x, g, b):
    mu = jnp.mean(x, axis=-1, keepdims=True)
    xc = x - mu
    return xc * lax.rsqrt(jnp.mean(xc * xc, axis=-1, keepdims=True) + EPS) * g + b


def _sigmoid(x):
    return 0.5 * jnp.tanh(0.5 * x) + 0.5


def _dot(a, b):
    return jnp.dot(a, b, preferred_element_type=F32)


def _mixer_in_kernel(x_ref, xh_ref, g1_ref, win_ref, cos_ref, sin_ref, qg_ref, kg_ref, bd32_ref,
                     poolw_ref, pools_ref, convw_ref, convb_ref, clng_ref, clnb_ref,
                     slng_ref, slnb_ref, sguw_ref, sgub_ref,
                     q_ref, k_ref, vt_ref, ya_ref, yc_ref, yd_ref,
                     a_scr, s2_scr, s4_scr, s8_scr, u_scr, us_scr):
    i = pl.program_id(1)
    tm = x_ref.shape[1]
    g1 = g1_ref[...]
    h = _rms(x_ref[0], g1).astype(BF16)
    proj = _dot(h, win_ref[0])

    hh = _rms(xh_ref[0], g1).astype(BF16)
    halo_ok = (i > 0).astype(F32)
    pa_h = _dot(hh, win_ref[0, :, 0:BR]) * halo_ok
    pc_h = _dot(hh, win_ref[0, :, 4 * BR:6 * BR]) * halo_ok

    a = proj[:, 0:BR]
    a_scr[0:HALO, :] = pa_h
    a_scr[HALO:HALO + tm, :] = a
    n_ext = tm + HALO
    s2_scr[8:n_ext, :] = a_scr[8:n_ext, :] + a_scr[7:n_ext - 1, :]
    s4_scr[16:n_ext, :] = s2_scr[16:n_ext, :] + s2_scr[14:n_ext - 2, :]
    s8_scr[24:n_ext, :] = s4_scr[24:n_ext, :] + s4_scr[20:n_ext - 4, :]
    s16 = s8_scr[HALO:n_ext, :] + s8_scr[HALO - 8:n_ext - 8, :]
    lane = lax.broadcasted_iota(jnp.int32, (tm, BR), 1)
    row = lax.broadcasted_iota(jnp.int32, (tm, BR), 0)
    win_sum = jnp.where(lane < POOL_GW, s2_scr[HALO:n_ext, :],
                        jnp.where(lane < 2 * POOL_GW, s4_scr[HALO:n_ext, :],
                                  jnp.where(lane < 3 * POOL_GW, s8_scr[HALO:n_ext, :], s16)))
    width = jnp.where(lane < POOL_GW, POOL_WINDOWS[0],
                      jnp.where(lane < 2 * POOL_GW, POOL_WINDOWS[1],
                                jnp.where(lane < 3 * POOL_GW, POOL_WINDOWS[2], POOL_WINDOWS[3])))
    cnt = jnp.minimum(width, i * tm + row + 1).astype(F32)
    pooled = win_sum / cnt - a
    ya_ref[0] = (_dot(pooled.astype(BF16), poolw_ref[...]) * pools_ref[...]).astype(BF16)

    lo_half = (lane % QK_DIM) < (QK_DIM // 2)

    def qk_norm_rope(t, g_ref, scale):
        ss = _dot((t * t).astype(BF16), bd32_ref[...])
        tn = t * lax.rsqrt(ss * (1.0 / QK_DIM) + EPS) * g_ref[...]
        rot = jnp.where(lo_half, pltpu.roll(tn, BR - QK_DIM // 2, 1), pltpu.roll(tn, QK_DIM // 2, 1))
        return (tn * cos_ref[...] + rot * sin_ref[...]) * scale

    q_ref[0] = qk_norm_rope(proj[:, BR:2 * BR], qg_ref, QK_DIM ** -0.5 * LOG2E).T.astype(BF16)
    kk = qk_norm_rope(proj[:, 2 * BR:3 * BR], kg_ref, 1.0).astype(BF16)
    for g in range(QK_GROUPS):
        k_ref[0, g] = kk[:, g * QK_DIM:(g + 1) * QK_DIM]
    vt_ref[0] = proj[:, 3 * BR:4 * BR].T.astype(BF16)

    u_scr[0:HALO, :] = pc_h[:, 0:BR] * _sigmoid(pc_h[:, BR:2 * BR])
    u_scr[HALO:HALO + tm, :] = proj[:, 4 * BR:5 * BR] * _sigmoid(proj[:, 5 * BR:6 * BR])
    taps = [convw_ref[j:j + 1, :] for j in range(CONV_W)]
    convb = convb_ref[...]
    clng = clng_ref[...]
    clnb = clnb_ref[...]
    n_sh = us_scr.shape[1]
    for s in range(1, 8):
        us_scr[s - 1] = u_scr[s:s + n_sh, :]
    rc = 64
    base = HALO - (CONV_W - 1)

    def tap_window(c, j):
        s, r0 = (base + j) % 8, c * rc + (base + j) // 8 * 8
        return u_scr[r0:r0 + rc, :] if s == 0 else us_scr[s - 1, r0:r0 + rc, :]

    for c in range(tm // rc):
        acc = convb + taps[0] * tap_window(c, 0)
        for j in range(1, CONV_W):
            acc = acc + taps[j] * tap_window(c, j)
        y = _ln(acc, clng, clnb)
        yc_ref[0, c * rc:(c + 1) * rc, :] = (y * _sigmoid(y)).astype(BF16)

    d_in = proj[:, 6 * BR:8 * BR]
    z = 0.5 * d_in * (1.0 + lax.erf(d_in * (2.0 ** -0.5)))
    vv = _ln(z[:, BR:2 * BR], slng_ref[...], slnb_ref[...]).astype(BF16)
    tr = lax.broadcasted_iota(jnp.int32, (SGU_LEN, SGU_LEN), 0)
    tc = lax.broadcasted_iota(jnp.int32, (SGU_LEN, SGU_LEN), 1)
    wtri = [jnp.where(tr >= tc, sguw_ref[g], 0.0).astype(BF16) for g in range(SGU_GROUPS)]
    lane_g = lax.broadcasted_iota(jnp.int32, (SGU_LEN, BR), 1) // (BR // SGU_GROUPS)
    sgub = sgub_ref[...]
    for n in range(tm // SGU_LEN):
        vn = vv[n * SGU_LEN:(n + 1) * SGU_LEN, :]
        sv = _dot(wtri[0], vn)
        for g in range(1, SGU_GROUPS):
            sv = jnp.where(lane_g == g, _dot(wtri[g], vn), sv)
        yd_ref[0, n * SGU_LEN:(n + 1) * SGU_LEN, :] = (
            z[n * SGU_LEN:(n + 1) * SGU_LEN, 0:BR] * (sv + sgub)).astype(BF16)


def _mixer_in(x3, g1, win, win_l, cos_t, sin_t, qg, kg, bd32, poolw, pools, convw, convb, clng, clnb,
              slng, slnb, sguw, sgub, tm):
    B, S, D = x3.shape
    nt = S // tm
    hb = tm // HALO
    full = lambda *shape: pl.BlockSpec(shape, lambda b, i: (0,) * len(shape))
    in_specs = [
        pl.BlockSpec((1, tm, D), lambda b, i: (b, i, 0)),
        pl.BlockSpec((1, HALO, D), lambda b, i: (b, jnp.maximum(i * hb - 1, 0), 0)),
        full(1, D), pl.BlockSpec((1, D, 8 * BR), lambda b, i: (win_l, 0, 0)),
        pl.BlockSpec((tm, BR), lambda b, i: (i, 0)),
        pl.BlockSpec((tm, BR), lambda b, i: (i, 0)),
        full(1, BR), full(1, BR), full(BR, BR),
        full(BR, BR), full(1, BR), full(CONV_W, BR), full(1, BR), full(1, BR), full(1, BR),
        full(1, BR), full(1, BR), full(SGU_GROUPS, SGU_LEN, SGU_LEN), full(SGU_LEN, BR),
    ]
    tok = lambda: pl.BlockSpec((1, tm, BR), lambda b, i: (b, i, 0))
    split = lambda: pl.BlockSpec((1, QK_GROUPS, tm, QK_DIM), lambda b, i: (b, 0, i, 0))
    flipped = lambda: pl.BlockSpec((1, BR, tm), lambda b, i: (b, 0, i))
    out_specs = [
        flipped(), split(), flipped(),
        tok(), tok(), tok(),
    ]
    out_shape = [
        jax.ShapeDtypeStruct((B, BR, S), BF16),
        jax.ShapeDtypeStruct((B, QK_GROUPS, S, QK_DIM), BF16),
        jax.ShapeDtypeStruct((B, BR, S), BF16),
        jax.ShapeDtypeStruct((B, S, BR), BF16),
        jax.ShapeDtypeStruct((B, S, BR), BF16),
        jax.ShapeDtypeStruct((B, S, BR), BF16),
    ]
    ext = pltpu.VMEM((tm + HALO, BR), F32)
    return pl.pallas_call(
        _mixer_in_kernel,
        grid=(B, nt),
        in_specs=in_specs,
        out_specs=out_specs,
        out_shape=out_shape,
        scratch_shapes=[ext, ext, ext, ext, ext, pltpu.VMEM((7, tm + HALO - 8, BR), F32)],
        compiler_params=pltpu.CompilerParams(
            dimension_semantics=("parallel", "parallel"), vmem_limit_bytes=V7X_VMEM_LIMIT),
        name="mixer_in",
    )(x3, x3, g1, win, cos_t, sin_t, qg, kg, bd32, poolw, pools, convw, convb, clng, clnb,
      slng, slnb, sguw, sgub)


def _attn_kernel(bounded_s, lam_ref, q_ref, k_ref, vt_ref, sg_ref, *rest, lam_init, tq, n_side):
    o_ref, p_scr = rest[n_side], rest[-1]
    _cast_side_blocks(rest[:n_side], rest[n_side + 1:2 * n_side + 1])
    i = pl.program_id(1)
    key_chunk = lax.broadcasted_iota(jnp.int32, (tq, tq), 0) // CHUNK
    qry_chunk = lax.broadcasted_iota(jnp.int32, (tq, tq), 1) // CHUNK
    diag_mask = key_chunk <= qry_chunk

    def scores(g, j, masked):
        k0 = pl.multiple_of(j * tq, tq)
        st = _dot(k_ref[0, g, pl.ds(k0, tq), :], q_ref[0, g * QK_DIM:(g + 1) * QK_DIM, :])
        return jnp.where(diag_mask, st, NEG_INF) if masked else st

    def fold8(x, op):
        return op(x.reshape(tq // 8, 8, tq), axis=0)

    def exact_maxima():
        def max_tile(j, ms, masked):
            return tuple(jnp.maximum(ms[g], fold8(scores(g, j, masked), jnp.max)) for g in range(QK_GROUPS))

        ms = tuple(jnp.full((8, tq), NEG_INF, F32) for _ in range(QK_GROUPS))
        ms = lax.fori_loop(0, i, functools.partial(max_tile, masked=False), ms)
        ms = max_tile(i, ms, True)
        return [jnp.max(x, axis=0, keepdims=True) for x in ms]

    def softmax_pv(shift):
        def acc_tile(j, carry, masked, buf=0):
            k0 = pl.multiple_of(j * tq, tq)
            ls = []
            for g in range(QK_GROUPS):
                st = scores(g, j, masked)
                p = jnp.exp2(st if shift is None else st - shift[g])
                p_scr[buf, g] = p.astype(BF16)
                ls.append(carry[g][0] + fold8(p, jnp.sum))
            out = []
            for g in range(QK_GROUPS):
                hd = g // 2
                vt = vt_ref[0, hd * V_DIM:(hd + 1) * V_DIM, pl.ds(k0, tq)]
                out.append((ls[g], carry[g][1] + _dot(vt, p_scr[buf, g])))
            return tuple(out)

        n_buf = p_scr.shape[0]

        def acc_group(jj, carry):
            for t in range(n_buf):
                carry = acc_tile(n_buf * jj + t, carry, False, t)
            return carry

        init = tuple((jnp.zeros((8, tq), F32), jnp.zeros((V_DIM, tq), F32)) for _ in range(QK_GROUPS))
        carry = lax.fori_loop(0, i // n_buf, acc_group, init)
        carry = lax.fori_loop(i - i % n_buf, i, functools.partial(acc_tile, masked=False), carry)
        carry = acc_tile(i, carry, True)
        return [(jnp.sum(l8, axis=0, keepdims=True), acc) for l8, acc in carry]

    def finish(res):
        lamv = lam_ref[...]
        lam = (jnp.exp(jnp.sum(lamv[0:1] * lamv[1:2], axis=-1, keepdims=True))
               - jnp.exp(jnp.sum(lamv[2:3] * lamv[3:4], axis=-1, keepdims=True)) + lam_init)
        sg = sg_ref[...]
        heads = []
        for hd in range(HEADS):
            o = res[2 * hd][1] / res[2 * hd][0] - lam * (res[2 * hd + 1][1] / res[2 * hd + 1][0])
            o = o * lax.rsqrt(jnp.mean(o * o, axis=0, keepdims=True) + EPS) * sg * (1.0 - lam_init)
            heads.append(o)
        o_ref[0] = jnp.concatenate(heads, axis=0).T.astype(BF16)

    @pl.when(bounded_s[0] == 1)
    def _():
        finish(softmax_pv(None))

    @pl.when(bounded_s[0] != 1)
    def _():
        finish(softmax_pv(exact_maxima()))


def _cast_rows_per_block(rows, steps):
    for rpb in range(ROW_ALIGN, rows + 1, ROW_ALIGN):
        if rows % rpb == 0 and rows // rpb <= steps:
            return rpb
    raise ValueError(f"no aligned row block for {rows} rows in {steps} steps")


def _side_cast_specs(side, steps, step_of):
    side2d = [w.reshape(-1, w.shape[-1]) for w in side]
    specs = []
    for w in side2d:
        rpb = _cast_rows_per_block(w.shape[0], steps)
        last = w.shape[0] // rpb - 1
        specs.append(pl.BlockSpec(
            (rpb, w.shape[1]), lambda *idx, last=last: (jnp.minimum(step_of(*idx), last), 0)))
    return side2d, specs


def _cast_side_blocks(in_refs, out_refs):
    for w_ref, wb_ref in zip(in_refs, out_refs):
        wb_ref[...] = w_ref[...].astype(BF16)


def _attention(bounded, lamv, q, k, vt, sg_col, lam_init, tq, side=()):
    B, _, S = q.shape
    nq = S // tq
    side2d, side_specs = _side_cast_specs(side, B * nq, lambda b, i, *_: b * nq + i)
    outs = pl.pallas_call(
        functools.partial(_attn_kernel, lam_init=lam_init, tq=tq, n_side=len(side2d)),
        grid_spec=pltpu.PrefetchScalarGridSpec(
            num_scalar_prefetch=1,
            grid=(B, nq),
            in_specs=[
                pl.BlockSpec((4, QK_DIM), lambda b, i, *_: (0, 0)),
                pl.BlockSpec((1, BR, tq), lambda b, i, *_: (b, 0, i)),
                pl.BlockSpec((1, QK_GROUPS, S, QK_DIM), lambda b, i, *_: (b, 0, 0, 0)),
                pl.BlockSpec((1, BR, S), lambda b, i, *_: (b, 0, 0)),
                pl.BlockSpec((V_DIM, tq), lambda b, i, *_: (0, 0)),
            ] + side_specs,
            out_specs=[pl.BlockSpec((1, tq, BR), lambda b, i, *_: (b, i, 0))] + side_specs,
            scratch_shapes=[pltpu.VMEM((KV_TILES_PER_STEP, QK_GROUPS, tq, tq), BF16)]),
        out_shape=[jax.ShapeDtypeStruct((B, S, BR), BF16)]
                  + [jax.ShapeDtypeStruct(w.shape, BF16) for w in side2d],
        compiler_params=pltpu.CompilerParams(
            dimension_semantics=("arbitrary", "arbitrary"), vmem_limit_bytes=V7X_VMEM_LIMIT),
        name="diff_attn",
    )(bounded, lamv, q, k, vt, sg_col, *side2d)
    return outs[0], [o.reshape(w.shape) for o, w in zip(outs[1:], side)]


def _merge_kernel(*refs, with_router, n_side):
    n_in = 12 if with_router else 11
    (x_ref, ya_ref, yb_ref, yc_ref, yd_ref, g1_ref, wg_ref, bg_ref, wb_ref, wo_ref, g2_ref) = refs[:11]
    rt_ref = refs[11] if with_router else None
    outs = refs[n_in + n_side:]
    x1_ref, h2_ref = outs[0], outs[1]
    gates_ref = outs[2] if with_router else None
    _cast_side_blocks(refs[n_in:n_in + n_side], outs[len(outs) - n_side:])
    x = x_ref[...]
    d = x.shape[1]
    h = _rms(x, g1_ref[...]).astype(BF16)
    merged = None
    for b, y_ref in enumerate((ya_ref, yb_ref, yc_ref, yd_ref)):
        gate = _sigmoid(_dot(h, wg_ref[0, :, b * d:(b + 1) * d]) + bg_ref[:, b * d:(b + 1) * d])
        term = gate * _dot(y_ref[...], wb_ref[0, b])
        merged = term if merged is None else merged + term
    x1 = x + _dot(merged.astype(BF16), wo_ref[0])
    x1_ref[...] = x1
    h2 = _rms(x1, g2_ref[...])
    h2_ref[...] = h2.astype(BF16)
    if with_router:
        def split(t):
            hi = t.astype(BF16)
            return hi, (t - hi.astype(F32)).astype(BF16)

        def nt_dot(a, b):
            return lax.dot_general(a, b, (((1,), (1,)), ((), ())), preferred_element_type=F32)

        (r_hi, r_lo), (h_hi, h_lo) = split(rt_ref[...]), split(h2)
        logits = nt_dot(r_hi, h_hi) + (nt_dot(r_hi, h_lo) + nt_dot(r_lo, h_hi))
        e_idx = lax.broadcasted_iota(jnp.int32, logits.shape, 0)
        v1 = jnp.max(logits, axis=0, keepdims=True)
        i1 = jnp.min(jnp.where(logits == v1, e_idx, N_EXPERTS), axis=0, keepdims=True)
        rest = jnp.where(e_idx == i1, -jnp.inf, logits)
        v2 = jnp.max(rest, axis=0, keepdims=True)
        i2 = jnp.min(jnp.where(rest == v2, e_idx, N_EXPERTS), axis=0, keepdims=True)
        w2 = 1.0 / (1.0 + jnp.exp(v1 - v2))
        gates_ref[...] = jnp.where(e_idx == i1, 1.0 - w2, jnp.where(e_idx == i2, w2, 0.0))


def _merge(x2, ya, yb, yc, yd, g1, wg, bg, wb, wo, layer, g2, router_t, tm, side=()):
    T, D = x2.shape
    with_router = router_t is not None
    side2d, side_specs = _side_cast_specs(side, T // tm, lambda i: i)
    full = lambda *shape: pl.BlockSpec(shape, lambda i: (0,) * len(shape))
    of_layer = lambda *shape: pl.BlockSpec((1,) + shape, lambda i: (layer,) + (0,) * len(shape))
    tokb = lambda w: pl.BlockSpec((tm, w), lambda i: (i, 0))
    in_specs = [tokb(D), tokb(BR), tokb(BR), tokb(BR), tokb(BR),
                full(1, D), of_layer(D, N_BRANCH * D), full(1, N_BRANCH * D), of_layer(N_BRANCH, BR, D),
                of_layer(D, D), full(1, D)]
    args = [x2, ya, yb, yc, yd, g1, wg, bg, wb, wo, g2]
    out_specs = [tokb(D), tokb(D)]
    out_shape = [jax.ShapeDtypeStruct((T, D), F32), jax.ShapeDtypeStruct((T, D), BF16)]
    if with_router:
        in_specs.append(full(N_EXPERTS, D))
        args.append(router_t)
        out_specs.append(pl.BlockSpec((N_EXPERTS, tm), lambda i: (0, i)))
        out_shape.append(jax.ShapeDtypeStruct((N_EXPERTS, T), F32))
    outs = pl.pallas_call(
        functools.partial(_merge_kernel, with_router=with_router, n_side=len(side2d)),
        grid=(T // tm,),
        in_specs=in_specs + side_specs,
        out_specs=out_specs + side_specs,
        out_shape=out_shape + [jax.ShapeDtypeStruct(w.shape, BF16) for w in side2d],
        compiler_params=pltpu.CompilerParams(
            dimension_semantics=("arbitrary",), vmem_limit_bytes=V7X_VMEM_LIMIT),
        name="merge_router" if with_router else "merge",
    )(*args, *side2d)
    n_main = len(out_shape)
    return outs[:n_main], [o.reshape(w.shape) for o, w in zip(outs[n_main:], side)]


def _ffn_kernel(x1_ref, h2_ref, w1_ref, w3_ref, w2_ref, o_ref, *, tf):
    h2 = h2_ref[...]
    acc = x1_ref[...]
    for c0 in range(0, w1_ref.shape[2], tf):
        c1 = min(c0 + tf, w1_ref.shape[2])
        a = _dot(h2, w1_ref[0, :, c0:c1])
        mid = a * _sigmoid(a) * _dot(h2, w3_ref[0, :, c0:c1])
        acc = acc + _dot(mid.astype(BF16), w2_ref[0, c0:c1, :])
    o_ref[...] = acc


def _ffn(x1, h2, w1, w3, w2, which, tm, tf):
    T, D = x1.shape
    F = w1.shape[2]
    resident = lambda *shape: pl.BlockSpec((1,) + shape, lambda i: (which, 0, 0), pipeline_mode=pl.Buffered(1))
    return pl.pallas_call(
        functools.partial(_ffn_kernel, tf=tf),
        grid=(T // tm,),
        in_specs=[pl.BlockSpec((tm, D), lambda i: (i, 0)),
                  pl.BlockSpec((tm, D), lambda i: (i, 0)),
                  resident(D, F), resident(D, F), resident(F, D)],
        out_specs=pl.BlockSpec((tm, D), lambda i: (i, 0)),
        out_shape=jax.ShapeDtypeStruct((T, D), F32),
        compiler_params=pltpu.CompilerParams(
            dimension_semantics=("parallel",), vmem_limit_bytes=V7X_VMEM_LIMIT),
        name="dense_ffn",
    )(x1, h2, w1, w3, w2)


ROW_ALIGN = 16
META_LANES = 128


def _moe_dims(T, tc, tm):
    n_ct = T // tc
    local_rows = 2 * tc + N_EXPERTS * ROW_ALIGN
    max_rows = 2 * T + n_ct * N_EXPERTS * (ROW_ALIGN - 1)
    n_mt = -(-max_rows // tm) + N_EXPERTS
    assert n_ct <= META_LANES and n_mt <= META_LANES and tm & (tm - 1) == 0
    return n_ct, local_rows, n_mt


def _sublane_excl_scan(x):
    rows, run = [], jnp.zeros_like(x[0:1, :])
    for e in range(x.shape[0]):
        rows.append(run)
        run = run + x[e:e + 1, :]
    return jnp.concatenate(rows, axis=0)


def _plan_kernel(gates_ref, member_ref, tri_ref, goff_ref, loc_ref, nch_ref, meta_ref, *, tm):
    routed = (gates_ref[...] > 0.0).astype(BF16)
    cnt = _dot(routed, member_ref[...]).astype(jnp.int32)
    row_sh, tm_sh = ROW_ALIGN.bit_length() - 1, tm.bit_length() - 1
    pad = ((cnt + (ROW_ALIGN - 1)) >> row_sh) << row_sh
    incl = jnp.dot(pad.astype(F32), tri_ref[...], preferred_element_type=F32,
                   precision=lax.Precision.HIGHEST).astype(jnp.int32)
    tot = jnp.max(incl, axis=1, keepdims=True)
    etot = ((tot + (tm - 1)) >> tm_sh) << tm_sh
    base = _sublane_excl_scan(etot)
    goff_ref[...] = base + incl - pad
    loc_ref[...] = _sublane_excl_scan(pad)
    nch_ref[...] = pad >> row_sh

    lane = lax.broadcasted_iota(jnp.int32, (N_EXPERTS, META_LANES), 1)
    sub = lax.broadcasted_iota(jnp.int32, (N_EXPERTS, META_LANES), 0)
    end = base + etot
    n_used = jnp.max(end, axis=0, keepdims=True) >> tm_sh
    src = jnp.minimum(lane, n_used - 1)
    expert = jnp.minimum(jnp.sum((src * tm >= end).astype(jnp.int32), axis=0, keepdims=True), N_EXPERTS - 1)
    start = lane * tm
    inside = (start >= base) & (start < end)
    valid = jnp.sum(jnp.where(inside, jnp.clip(tot - (start - base), 0, tm), 0), axis=0, keepdims=True)
    fill_start = jnp.sum(jnp.where(sub == lane, base + tot, 0), axis=0, keepdims=True)
    fill_nch = jnp.sum(jnp.where(sub == lane, (etot - tot) >> row_sh, 0), axis=0, keepdims=True)
    meta = jnp.where(sub == 0, expert, jnp.where(sub == 1, valid, jnp.where(sub == 2, src,
                     jnp.where(sub == 3, fill_start, jnp.where(sub == 4, fill_nch,
                     jnp.where(sub == 5, n_used, 0))))))
    meta_ref[...] = meta


def _moe_plan(gates_t, tc, tm):
    E, T = gates_t.shape
    member = (jnp.arange(T)[:, None] // tc == jnp.arange(META_LANES)[None, :]).astype(BF16)
    tri = (jnp.arange(META_LANES)[:, None] <= jnp.arange(META_LANES)[None, :]).astype(F32)
    tbl = jax.ShapeDtypeStruct((N_EXPERTS, META_LANES), jnp.int32)
    return pl.pallas_call(
        functools.partial(_plan_kernel, tm=tm),
        out_shape=[tbl, tbl, tbl, tbl],
        compiler_params=pltpu.CompilerParams(vmem_limit_bytes=V7X_VMEM_LIMIT),
        name="moe_plan",
    )(gates_t, member, tri)


def _segment_rows(gates, loc_s, tile, strict_upper):
    routed = gates > 0.0
    pos = _dot(routed.astype(BF16), strict_upper).astype(jnp.int32)
    sub = lax.broadcasted_iota(jnp.int32, gates.shape, 0)
    loc = jnp.zeros(gates.shape, jnp.int32)
    for e in range(N_EXPERTS):
        loc = jnp.where(sub == e, loc_s[e, tile], loc)
    rows = loc + pos
    row_a = jnp.min(jnp.where(routed, rows, jnp.iinfo(jnp.int32).max), axis=0, keepdims=True)
    row_b = jnp.max(jnp.where(routed, rows, -1), axis=0, keepdims=True)
    gate_a = jnp.sum(jnp.where(routed & (rows == row_a), gates, 0.0), axis=0, keepdims=True)
    gate_b = jnp.sum(jnp.where(routed & (rows == row_b), gates, 0.0), axis=0, keepdims=True)
    return row_a, row_b, gate_a, gate_b


def _segment_copies(src, dst, src_row, dst_row, n_chunks, max_chunks, sem, start, stream=0):
    done = 0
    piece = 1 << (max_chunks.bit_length() - 1)
    while piece:
        take = n_chunks & piece
        rows = piece * ROW_ALIGN
        priority = (stream + piece.bit_length()) % 2

        @pl.when(take != 0)
        def _(done=done, rows=rows, priority=priority):
            def at(row0):
                row = row0 + done * ROW_ALIGN
                return pl.ds(row if isinstance(row, int) else pl.multiple_of(row, ROW_ALIGN), rows)

            copy = pltpu.make_async_copy(src.at[at(src_row)], dst.at[at(dst_row)], sem)
            copy.start(priority=priority) if start else copy.wait()

        done = done + take
        piece //= 2


def _dispatch_kernel(goff_s, loc_s, nch_s, meta_s, gates_ref, h2_ref, upper_ref, xs_hbm,
                     xc_scr, zero_scr, sem, tile_sem):
    i = pl.program_id(0)
    n_ct = pl.num_programs(0)
    row_a, row_b, _, _ = _segment_rows(gates_ref[...], loc_s, i, upper_ref[...])
    r_iota = lax.broadcasted_iota(jnp.int32, (xc_scr.shape[1], gates_ref.shape[1]), 0)
    onehot = jnp.where(r_iota == row_a, 1.0, jnp.where(r_iota == row_b, 1.0, 0.0)).astype(BF16)
    slot = i % 2
    xc_scr[slot] = _dot(onehot, h2_ref[...]).astype(BF16)

    seg_chunks = gates_ref.shape[1] // ROW_ALIGN + 1

    def segments(step, buf, start):
        for e in range(N_EXPERTS):
            _segment_copies(xc_scr.at[buf], xs_hbm, loc_s[e, step], goff_s[e, step], nch_s[e, step],
                            seg_chunks, sem.at[buf], start, stream=e)

    segments(i, slot, True)

    @pl.when(i > 0)
    def _():
        segments(i - 1, 1 - slot, False)

    @pl.when(i == n_ct - 1)
    def _():
        segments(i, slot, False)
        zero_scr[...] = jnp.zeros(zero_scr.shape, zero_scr.dtype)
        tm = zero_scr.shape[0]
        for start in (True, False):
            for e in range(N_EXPERTS):
                _segment_copies(zero_scr, xs_hbm, 0, meta_s[3, e], meta_s[4, e], tm // ROW_ALIGN - 1,
                                sem.at[0], start, stream=e)

        n_mt = xs_hbm.shape[0] // tm

        def tile_copy(t):
            return pltpu.make_async_copy(zero_scr, xs_hbm.at[pl.ds(pl.multiple_of(t * tm, tm), tm)], tile_sem)

        def fill_tile(t, carry):
            tile_copy(t).start()
            return carry
        lax.fori_loop(meta_s[5, 0], n_mt, fill_tile, 0)

        def drain_tile(t, carry):
            tile_copy(t).wait()
            return carry
        lax.fori_loop(meta_s[5, 0], n_mt, drain_tile, 0)


def _moe_dispatch(tables, gates_t, h2, tc, tm):
    T, D = h2.shape
    n_ct, local_rows, n_mt = _moe_dims(T, tc, tm)
    upper = (jnp.arange(tc)[:, None] < jnp.arange(tc)[None, :]).astype(BF16)
    return pl.pallas_call(
        _dispatch_kernel,
        grid_spec=pltpu.PrefetchScalarGridSpec(
            num_scalar_prefetch=4,
            grid=(n_ct,),
            in_specs=[pl.BlockSpec((N_EXPERTS, tc), lambda i, *_: (0, i)),
                      pl.BlockSpec((tc, D), lambda i, *_: (i, 0)),
                      pl.BlockSpec((tc, tc), lambda i, *_: (0, 0))],
            out_specs=pl.BlockSpec(memory_space=pl.ANY),
            scratch_shapes=[pltpu.VMEM((2, local_rows, D), BF16), pltpu.VMEM((tm, D), BF16),
                            pltpu.SemaphoreType.DMA((2,)), pltpu.SemaphoreType.DMA(())]),
        out_shape=jax.ShapeDtypeStruct((n_mt * tm, D), BF16),
        compiler_params=pltpu.CompilerParams(
            dimension_semantics=("arbitrary",), vmem_limit_bytes=V7X_VMEM_LIMIT),
        name="moe_dispatch",
    )(*tables, gates_t, h2, upper)


def _experts_kernel(meta_s, xs_ref, w1_ref, w3_ref, w2_ref, ys_ref, acc_scr):
    m = pl.program_id(0)
    f = pl.program_id(1)
    last_f = f == pl.num_programs(1) - 1
    valid = meta_s[1, m]
    tm = xs_ref.shape[0]
    half = tm // 2

    def ffn_rows(rows):
        x = xs_ref[0:rows, :]
        a = _dot(x, w1_ref[0])
        mid = a * _sigmoid(a) * _dot(x, w3_ref[0])
        part = _dot(mid.astype(BF16), w2_ref[0])

        @pl.when(f == 0)
        def _():
            acc_scr[0:rows, :] = part

        @pl.when(f > 0)
        def _():
            acc_scr[0:rows, :] += part

        @pl.when(last_f)
        def _():
            ys_ref[0:rows, :] = acc_scr[0:rows, :].astype(BF16)
            if rows < tm:
                ys_ref[rows:tm, :] = jnp.zeros((tm - rows, ys_ref.shape[1]), ys_ref.dtype)

    @pl.when(valid > half)
    def _():
        ffn_rows(tm)

    @pl.when((valid > 0) & (valid <= half))
    def _():
        ffn_rows(half)

    @pl.when((valid == 0) & last_f)
    def _():
        ys_ref[...] = jnp.zeros(ys_ref.shape, ys_ref.dtype)


def _moe_experts(meta, xs, w1, w3, w2, which, tm, tf):
    rows, D = xs.shape
    F = w1.shape[3]
    nf = F // tf
    w1, w3, w2 = (w.reshape((-1,) + w.shape[2:]) for w in (w1, w3, w2))
    meta = meta.at[0].add(which * N_EXPERTS)

    def f_eff(m, f, meta):
        return jnp.where(meta[1, m] > 0, f, nf - 1)

    return pl.pallas_call(
        _experts_kernel,
        grid_spec=pltpu.PrefetchScalarGridSpec(
            num_scalar_prefetch=1,
            grid=(rows // tm, nf),
            in_specs=[pl.BlockSpec((tm, D), lambda m, f, meta: (meta[2, m], 0)),
                      pl.BlockSpec((1, D, tf), lambda m, f, meta: (meta[0, m], 0, f_eff(m, f, meta))),
                      pl.BlockSpec((1, D, tf), lambda m, f, meta: (meta[0, m], 0, f_eff(m, f, meta))),
                      pl.BlockSpec((1, tf, D), lambda m, f, meta: (meta[0, m], f_eff(m, f, meta), 0))],
            out_specs=pl.BlockSpec((tm, D), lambda m, f, meta: (m, 0)),
            scratch_shapes=[pltpu.VMEM((tm, D), F32)]),
        out_shape=jax.ShapeDtypeStruct((rows, D), BF16),
        compiler_params=pltpu.CompilerParams(
            dimension_semantics=("arbitrary", "arbitrary"), vmem_limit_bytes=V7X_VMEM_LIMIT),
        name="moe_experts",
    )(meta, xs, w1, w3, w2)


def _combine_kernel(goff_s, loc_s, nch_s, gates_ref, x1_ref, upper_ref, ys_hbm, o_ref, yc_scr, sem):
    i = pl.program_id(0)
    tc = gates_ref.shape[1]

    slot = i % 2

    seg_chunks = tc // ROW_ALIGN + 1

    def segments(step, buf, start):
        for e in range(N_EXPERTS):
            _segment_copies(ys_hbm, yc_scr.at[buf], goff_s[e, step], loc_s[e, step], nch_s[e, step],
                            seg_chunks, sem.at[buf], start, stream=e)

    @pl.when(i == 0)
    def _():
        yc_scr[...] = jnp.zeros(yc_scr.shape, yc_scr.dtype)
        segments(i, slot, True)

    @pl.when(i + 1 < pl.num_programs(0))
    def _():
        segments(i + 1, 1 - slot, True)

    row_a, row_b, gate_a, gate_b = _segment_rows(gates_ref[...], loc_s, i, upper_ref[...])
    stats = jnp.concatenate([row_a.astype(F32), row_b.astype(F32), gate_a, gate_b,
                             jnp.zeros((META_LANES - 4, tc), F32)], axis=0).T
    lane = lax.broadcasted_iota(jnp.int32, (tc, yc_scr.shape[1]), 1).astype(F32)
    sel = jnp.where(lane == stats[:, 0:1], stats[:, 2:3],
                    jnp.where(lane == stats[:, 1:2], stats[:, 3:4], 0.0)).astype(BF16)

    segments(i, slot, False)
    o_ref[...] = x1_ref[...] + _dot(sel, yc_scr[slot])


def _moe_combine(tables, gates_t, x1, ys, tc, tm):
    T, D = x1.shape
    n_ct, local_rows, _ = _moe_dims(T, tc, tm)
    upper = (jnp.arange(tc)[:, None] < jnp.arange(tc)[None, :]).astype(BF16)
    return pl.pallas_call(
        _combine_kernel,
        grid_spec=pltpu.PrefetchScalarGridSpec(
            num_scalar_prefetch=3,
            grid=(n_ct,),
            in_specs=[pl.BlockSpec((N_EXPERTS, tc), lambda i, *_: (0, i)),
                      pl.BlockSpec((tc, D), lambda i, *_: (i, 0)),
                      pl.BlockSpec((tc, tc), lambda i, *_: (0, 0)),
                      pl.BlockSpec(memory_space=pl.ANY)],
            out_specs=pl.BlockSpec((tc, D), lambda i, *_: (i, 0)),
            scratch_shapes=[pltpu.VMEM((2, local_rows, D), BF16), pltpu.SemaphoreType.DMA((2,))]),
        out_shape=jax.ShapeDtypeStruct((T, D), F32),
        compiler_params=pltpu.CompilerParams(
            dimension_semantics=("arbitrary",), vmem_limit_bytes=V7X_VMEM_LIMIT),
        name="moe_combine",
    )(*tables[:3], gates_t, x1, upper, ys)


def _moe(x1, h2, gates_t, w1, w3, w2, which, tc, tm, tf):
    goff, loc, nch, meta = _moe_plan(gates_t, tc, tm)
    xs = _moe_dispatch((goff, loc, nch, meta), gates_t, h2, tc, tm)
    ys = _moe_experts(meta, xs, w1, w3, w2, which, tm, tf)
    return _moe_combine((goff, loc, nch), gates_t, x1, ys, tc, tm)


def _rope_tables(seq):
    pos = jnp.arange(seq, dtype=F32)
    inv = 1.0 / (ROPE_THETA ** (jnp.arange(0, QK_DIM, 2, dtype=F32) / QK_DIM))
    ang = pos[:, None] * inv[None, :]
    cos, sin = jnp.cos(ang), jnp.sin(ang)
    groups = BR // QK_DIM
    cos_t = jnp.tile(jnp.concatenate([cos, cos], axis=-1), (1, groups))
    sin_t = jnp.tile(jnp.concatenate([-sin, sin], axis=-1), (1, groups))
    return cos_t, sin_t


def _block_diag(blocks):
    g, n, m = blocks.shape
    eye = jnp.eye(g, dtype=blocks.dtype)
    return (eye[:, None, :, None] * blocks[:, :, None, :]).reshape(g * n, g * m)


def kernel(x, norm1_g, w_in, w_gate, b_gate, w_branch, w_out, pool_w, pool_scale, attn_qn_g, attn_kn_g,
           lam_q1, lam_k1, lam_q2, lam_k2, attn_subln_g, conv_w, conv_b, conv_ln_g, conv_ln_b,
           sgu_ln_g, sgu_ln_b, sgu_w, sgu_b, norm2_g, ffn_w1, ffn_w3, ffn_w2, moe_router,
           moe_w1, moe_w3, moe_w2):
    B, S, D = x.shape
    depth = w_in.shape[0]
    T = B * S
    tm_mix = min(1024, S)
    tq = min(256, S)
    tm_merge = min(512, T)
    tm_ffn = min(1024, T)
    tm_moe = 512

    cos_t, sin_t = _rope_tables(S)
    bd32 = _block_diag(jnp.ones((BR // QK_DIM, QK_DIM, QK_DIM), BF16))
    row = lambda v: v.reshape(1, -1).astype(F32)

    stacked = {"w_in": w_in, "w_gate": w_gate, "w_branch": w_branch, "w_out": w_out, "ffn_w1": ffn_w1,
               "ffn_w3": ffn_w3, "ffn_w2": ffn_w2}
    moe_hosts = {0: {"moe_w1": moe_w1, "moe_w3": moe_w3}, 1: {"moe_w2": moe_w2}} if depth > 1 else {}
    first_in = w_in[0:1].astype(BF16)
    bf = {}

    for l in range(depth):
        x3 = x.reshape(B, S, D)
        q, k, vt, ya, yc, yd = _mixer_in(
            x3, row(norm1_g[l]), first_in if l == 0 else bf["w_in"], 0 if l == 0 else l, cos_t, sin_t,
            row(jnp.tile(attn_qn_g[l], BR // QK_DIM)), row(jnp.tile(attn_kn_g[l], BR // QK_DIM)), bd32,
            _block_diag(pool_w[l]).astype(BF16), row(pool_scale[l]),
            conv_w[l], row(conv_b[l]), row(conv_ln_g[l]), row(conv_ln_b[l]),
            row(sgu_ln_g[l]), row(sgu_ln_b[l]), sgu_w[l],
            jnp.repeat(sgu_b[l].T, BR // SGU_GROUPS, axis=1), tm_mix)
        lamv = jnp.stack([lam_q1[l], lam_k1[l], lam_q2[l], lam_k2[l]]).astype(F32)
        lam_init = 0.8 - 0.6 * math.exp(-0.3 * l)
        sg_col = jnp.broadcast_to(attn_subln_g[l].astype(F32)[:, None], (V_DIM, tq))
        score_bound = (QK_DIM ** 0.5 * LOG2E) * jnp.max(jnp.abs(attn_qn_g[l])) * jnp.max(jnp.abs(attn_kn_g[l]))
        bounded = (score_bound <= SAFE_SCORE_BOUND).astype(jnp.int32).reshape(1)
        yb, cast = _attention(bounded, lamv, q, k, vt, sg_col, lam_init, tq,
                              side=list(stacked.values()) if l == 0 else ())
        if l == 0:
            bf.update(zip(stacked, cast))

        is_moe = l % 2 == 1
        router_t = moe_router[l // 2].T.astype(F32) if is_moe else None
        flat = lambda y: y.reshape(T, BR)
        hosted = moe_hosts.get(l, {})
        outs, cast = _merge(x.reshape(T, D), flat(ya), flat(yb), flat(yc), flat(yd), row(norm1_g[l]),
                            bf["w_gate"], row(b_gate[l]), bf["w_branch"], bf["w_out"], l, row(norm2_g[l]),
                            router_t, tm_merge, side=list(hosted.values()))
        bf.update(zip(hosted, cast))
        i = l // 2
        if is_moe:
            x1, h2, gates_t = outs
            x = _moe(x1, h2, gates_t, bf["moe_w1"], bf["moe_w3"], bf["moe_w2"], i,
                     tm_merge, tm_moe, moe_w1.shape[3] // 2)
        else:
            x1, h2 = outs
            x = _ffn(x1, h2, bf["ffn_w1"], bf["ffn_w3"], bf["ffn_w2"], i, tm_ffn, FFN_CHUNK)
    return x.reshape(B, S, D)
```
